```python
import numpy as np
import jax
import jax.numpy as jnp
from jax import lax


D_MODEL = 4096
BATCH = 4
SEQ = 4096
DEPTH = 2

GRID_W = 64
CTX_LEN = 256
EPS = 1e-6
NEG_INF = -1e30
ROPE_BASE = 10000.0
ATTN_QBLOCK = 128

RET_HEADS = 8
RET_DK = 256
RET_DV = 256
RET_CHUNK = 128
NA_HEADS = 16
NA_HD = 128
NA_KH = 8
NA_KW = 16
NA_QB = 16
NA_KB = NA_QB + NA_KW
MLA_HEADS = 16
MLA_Q_RANK = 1536
MLA_KV_RANK = 512
MLA_NOPE = 128
MLA_ROPE = 64
MLA_V = 128
LRU_WIDTH = 2048
LRU_BLOCKS = 16
LRU_BD = LRU_WIDTH // LRU_BLOCKS
LRU_CONV = 4
LRU_C = 8.0
FFN_DENSE = 11008
N_EXPERTS = 8
TOP_K = 2
FFN_EXPERT = 4096

N_EVEN = (DEPTH + 1) // 2
N_ODD = DEPTH // 2
EVEN_SPLITS = (RET_HEADS * RET_DK, RET_HEADS * RET_DK, RET_HEADS * RET_DV, RET_HEADS * RET_DV, NA_HEADS * NA_HD, NA_HEADS * NA_HD, NA_HEADS * NA_HD)
EVEN_IN = sum(EVEN_SPLITS)
EVEN_OUT = RET_HEADS * RET_DV + NA_HEADS * NA_HD
ODD_SPLITS = (MLA_Q_RANK, MLA_KV_RANK, MLA_ROPE, LRU_WIDTH, LRU_WIDTH)
ODD_IN = sum(ODD_SPLITS)
ODD_OUT = MLA_HEADS * MLA_V + LRU_WIDTH

kernel_name = 'hybrid_retention_na_mla_rglru_dit_block'


def _rms_norm(x, g):
    xf = x.astype(jnp.float32)
    y = xf * lax.rsqrt(jnp.mean(xf * xf, axis=-1, keepdims=True) + EPS)
    return (y * g.astype(jnp.float32)).astype(x.dtype)


def _split(t, sizes):
    return jnp.split(t, [int(v) for v in np.cumsum(sizes)[:-1]], axis=-1)


def _heads(t, n):
    b, l, _ = t.shape
    return t.reshape(b, l, n, -1).transpose(0, 2, 1, 3)


def _merge(t):
    b, h, l, d = t.shape
    return t.transpose(0, 2, 1, 3).reshape(b, l, h * d)


def _axial_rope(n, rot_dim):
    t = jnp.arange(n)
    row = (t // GRID_W).astype(jnp.float32)
    col = (t % GRID_W).astype(jnp.float32)
    nf = rot_dim // 4
    inv = ROPE_BASE ** (-jnp.arange(nf, dtype=jnp.float32) / nf)
    ar = row[:, None] * inv
    ac = col[:, None] * inv
    return (jnp.cos(ar), jnp.sin(ar), jnp.cos(ac), jnp.sin(ac))


def _apply_rope(x, tab):
    cr, sr, cc, sc = tab
    x1, x2, x3, x4 = jnp.split(x.astype(jnp.float32), 4, axis=-1)
    out = jnp.concatenate([x1 * cr - x2 * sr, x2 * cr + x1 * sr, x3 * cc - x4 * sc, x4 * cc + x3 * sc], axis=-1)
    return out.astype(x.dtype)


def _blocked_attention(q, k, v, scale):
    b, h, n, dq = q.shape
    nb = n // ATTN_QBLOCK
    qb = jnp.moveaxis(q.reshape(b, h, nb, ATTN_QBLOCK, dq), 2, 0)

    def one_block(qblk):
        s = jnp.einsum('bhqd,bhkd->bhqk', qblk, k).astype(jnp.float32) * scale
        p = jax.nn.softmax(s, axis=-1).astype(v.dtype)
        return jnp.einsum('bhqk,bhkd->bhqd', p, v)

    o = lax.map(one_block, qb)
    return jnp.moveaxis(o, 0, 2).reshape(b, h, n, v.shape[-1])


def _retention_scan(q, k, v, log_gamma, state0):
    b, h, n, _ = q.shape
    nc = n // RET_CHUNK
    pos = jnp.arange(RET_CHUNK, dtype=jnp.float32)
    lg = log_gamma[:, None]
    diff = pos[:, None] - pos[None, :]
    intra = jnp.where(diff >= 0, jnp.exp(log_gamma[:, None, None] * jnp.maximum(diff, 0.0)), 0.0)
    q_dec = jnp.exp(lg * (pos + 1.0))[None, :, :, None]
    k_dec = jnp.exp(lg * (RET_CHUNK - 1.0 - pos))[None, :, :, None]
    s_dec = jnp.exp(log_gamma * RET_CHUNK)[None, :, None, None]

    def chunks(t):
        return jnp.moveaxis(t.astype(jnp.float32).reshape(b, h, nc, RET_CHUNK, t.shape[-1]), 2, 0)

    def step(state, qkv):
        qc, kc, vc = qkv
        att = jnp.einsum('bhid,bhjd->bhij', qc, kc) * intra
        out = jnp.einsum('bhij,bhjv->bhiv', att, vc) + jnp.einsum('bhid,bhdv->bhiv', qc * q_dec, state)
        state = state * s_dec + jnp.einsum('bhjd,bhjv->bhdv', kc * k_dec, vc)
        return state, out

    state, out = lax.scan(step, state0, (chunks(q), chunks(k), chunks(v)))
    return jnp.moveaxis(out, 0, 2).reshape(b, h, n, -1), state


def _retention_bidir(q_c, k_c, v_c, q_l, k_l, v_l, log_gamma):
    b, h = q_c.shape[:2]
    zero = jnp.zeros((b, h, RET_DK, RET_DV), jnp.float32)
    fl = lambda t: jnp.flip(t, axis=2)
    oc_f, sc_f = _retention_scan(q_c, k_c, v_c, log_gamma[0], zero)
    ol_f, _ = _retention_scan(q_l, k_l, v_l, log_gamma[0], sc_f)
    oc_b, sc_b = _retention_scan(fl(q_c), fl(k_c), fl(v_c), log_gamma[1], zero)
    ol_b, _ = _retention_scan(fl(q_l), fl(k_l), fl(v_l), log_gamma[1], sc_b)
    return oc_f + fl(oc_b), ol_f + fl(ol_b)


def _head_rms(y):
    return y * lax.rsqrt(jnp.mean(y * y, axis=-1, keepdims=True) + EPS)


def _na_latent(q, k, v, k_ctx, v_ctx, rpb):
    b, h, s, d = q.shape
    rows = s // GRID_W
    kh = min(NA_KH, rows)
    nqb = GRID_W // NA_QB
    qcol = np.arange(GRID_W).reshape(nqb, NA_QB)
    kstart = np.clip(qcol[:, 0] - NA_KW // 2, 0, GRID_W - NA_KB)
    kcol = kstart[:, None] + np.arange(NA_KB)
    cstart = np.clip(qcol - NA_KW // 2, 0, GRID_W - NA_KW)
    col_ok = (kcol[:, None, :] >= cstart[:, :, None]) & (kcol[:, None, :] < cstart[:, :, None] + NA_KW)
    dc_idx = np.clip(kcol[:, None, :] - qcol[:, :, None] + NA_KW - 1, 0, 2 * NA_KW - 2)
    rpb_cols = rpb[:, :, dc_idx]
    mask = col_ok[:, :, None, :]
    scale = d ** -0.5
    kg = k.reshape(b, h, rows, GRID_W, d)
    vg = v.reshape(b, h, rows, GRID_W, d)
    qg = jnp.moveaxis(q.reshape(b, h, rows, nqb, NA_QB, d), 2, 0)
    n_loc = kh * NA_KB

    def one_row(args):
        q_row, r = args
        r0 = jnp.clip(r - kh // 2, 0, rows - kh)
        k_blk = lax.dynamic_slice_in_dim(kg, r0, kh, axis=2)[:, :, :, kcol]
        v_blk = lax.dynamic_slice_in_dim(vg, r0, kh, axis=2)[:, :, :, kcol]
        s_loc = jnp.einsum('bhnqd,bhrnkd->bhnqrk', q_row, k_blk).astype(jnp.float32) * scale
        bias = jnp.take(rpb_cols, r0 + jnp.arange(kh) - r + NA_KH - 1, axis=1).astype(jnp.float32)
        s_loc = jnp.where(mask, s_loc + jnp.transpose(bias, (0, 2, 3, 1, 4)), NEG_INF)
        s_ctx = jnp.einsum('bhnqd,bhld->bhnql', q_row, k_ctx).astype(jnp.float32) * scale
        p = jax.nn.softmax(jnp.concatenate([s_loc.reshape(b, h, nqb, NA_QB, n_loc), s_ctx], axis=-1), axis=-1).astype(v.dtype)
        p_loc = p[..., :n_loc].reshape(b, h, nqb, NA_QB, kh, NA_KB)
        return jnp.einsum('bhnqrk,bhrnkd->bhnqd', p_loc, v_blk) + jnp.einsum('bhnql,bhld->bhnqd', p[..., n_loc:], v_ctx)

    out = lax.map(one_row, (qg, jnp.arange(rows)))
    return jnp.moveaxis(out, 0, 2).reshape(b, h, s, d)


def _even_mixer(hc, hl, w_in, w_out, ret_decay, rpb, rope_ret, need_ctx):
    log_g = jax.nn.log_sigmoid(ret_decay.astype(jnp.float32))
    rq_c, rk_c, rv_c, rg_c, nq_c, nk_c, nv_c = _split(hc @ w_in, EVEN_SPLITS)
    rq_l, rk_l, rv_l, rg_l, nq_l, nk_l, nv_l = _split(hl @ w_in, EVEN_SPLITS)
    kscale = RET_DK ** -0.5
    q_c, k_c, v_c = _heads(rq_c, RET_HEADS), _heads(rk_c, RET_HEADS) * kscale, _heads(rv_c, RET_HEADS)
    q_l = _apply_rope(_heads(rq_l, RET_HEADS), rope_ret)
    k_l = _apply_rope(_heads(rk_l, RET_HEADS), rope_ret) * kscale
    v_l = _heads(rv_l, RET_HEADS)
    ret_c, ret_l = _retention_bidir(q_c, k_c, v_c, q_l, k_l, v_l, log_g)
    nkc, nvc = _heads(nk_c, NA_HEADS), _heads(nv_c, NA_HEADS)
    na_l = _na_latent(_heads(nq_l, NA_HEADS), _heads(nk_l, NA_HEADS), _heads(nv_l, NA_HEADS), nkc, nvc, rpb)
    ret_out_l = _merge(_head_rms(ret_l)).astype(hl.dtype) * jax.nn.silu(rg_l)
    out_l = jnp.concatenate([ret_out_l, _merge(na_l)], axis=-1) @ w_out
    out_c = None
    if need_ctx:
        ret_out_c = _merge(_head_rms(ret_c)).astype(hc.dtype) * jax.nn.silu(rg_c)
        na_c = _blocked_attention(_heads(nq_c, NA_HEADS), nkc, nvc, NA_HD ** -0.5)
        out_c = jnp.concatenate([ret_out_c, _merge(na_c)], axis=-1) @ w_out
    return out_c, out_l


def _mla_q(cq, gq, wuq, rope):
    q = _heads(_rms_norm(cq, gq) @ wuq, MLA_HEADS)
    q_nope, q_pe = q[..., :MLA_NOPE], q[..., MLA_NOPE:]
    if rope is not None:
        q_pe = _apply_rope(q_pe, rope)
    return jnp.concatenate([q_nope, q_pe], axis=-1)


def _mla_kv(ckv, kr, gkv, wukv, rope):
    kv = _heads(_rms_norm(ckv, gkv) @ wukv, MLA_HEADS)
    k_pe = kr[:, None]
    if rope is not None:
        k_pe = _apply_rope(k_pe, rope)
    b, h, n, _ = kv.shape
    k = jnp.concatenate([kv[..., :MLA_NOPE], jnp.broadcast_to(k_pe, (b, h, n, MLA_ROPE))], axis=-1)
    return k, kv[..., MLA_NOPE:]


def _conv_centred(x, w, bias):
    y = lax.conv_general_dilated(x, w[:, None, :].astype(x.dtype), (1,), [((LRU_CONV - 1) // 2, LRU_CONV // 2)],
                                 dimension_numbers=('NWC', 'WIO', 'NWC'), feature_group_count=x.shape[-1])
    return y + bias


def _lin_comb(l, r):
    return l[0] * r[0], r[0] * l[1] + r[1]


def _rglru_dir(x, wa, ba, wi, bi, lam, h0):
    b, n, ch = x.shape
    xb = x.reshape(b, n, LRU_BLOCKS, LRU_BD)
    r = jax.nn.sigmoid((jnp.einsum('bnkc,kcd->bnkd', xb, wa).reshape(b, n, ch) + ba).astype(jnp.float32))
    ig = jax.nn.sigmoid((jnp.einsum('bnkc,kcd->bnkd', xb, wi).reshape(b, n, ch) + bi).astype(jnp.float32))
    log_a = -LRU_C * r * jax.nn.softplus(-lam.astype(jnp.float32))
    a = jnp.exp(log_a)
    u = jnp.sqrt(-jnp.expm1(2.0 * log_a)) * ig * x.astype(jnp.float32)
    acc_a, acc_u = lax.associative_scan(_lin_comb, (a, u), axis=1)
    hseq = acc_a * h0[:, None, :] + acc_u
    return hseq, hseq[:, -1]


def _rglru_bidir(xc, xl, wa, ba, wi, bi, lam):
    h0 = jnp.zeros((xc.shape[0], xc.shape[-1]), jnp.float32)
    fl = lambda t: jnp.flip(t, axis=1)
    hc_f, sc_f = _rglru_dir(xc, wa[0], ba[0], wi[0], bi[0], lam[0], h0)
    hl_f, _ = _rglru_dir(xl, wa[0], ba[0], wi[0], bi[0], lam[0], sc_f)
    hc_b, sc_b = _rglru_dir(fl(xc), wa[1], ba[1], wi[1], bi[1], lam[1], h0)
    hl_b, _ = _rglru_dir(fl(xl), wa[1], ba[1], wi[1], bi[1], lam[1], sc_b)
    return hc_f + fl(hc_b), hl_f + fl(hl_b)


def _odd_mixer(hc, hl, w_in, gq, wuq, gkv, wukv, conv_w, conv_b, wa, ba, wi, bi, lam, w_out, rope_mla, need_ctx):
    cq_c, ckv_c, kr_c, lx_c, ly_c = _split(hc @ w_in, ODD_SPLITS)
    cq_l, ckv_l, kr_l, lx_l, ly_l = _split(hl @ w_in, ODD_SPLITS)
    scale = (MLA_NOPE + MLA_ROPE) ** -0.5
    k_c, v_c = _mla_kv(ckv_c, kr_c, gkv, wukv, None)
    k_l, v_l = _mla_kv(ckv_l, kr_l, gkv, wukv, rope_mla)
    q_l = _mla_q(cq_l, gq, wuq, rope_mla)
    att_l = _blocked_attention(q_l, jnp.concatenate([k_c, k_l], axis=2), jnp.concatenate([v_c, v_l], axis=2), scale)
    rc, rl = _rglru_bidir(_conv_centred(lx_c, conv_w, conv_b), _conv_centred(lx_l, conv_w, conv_b), wa, ba, wi, bi, lam)
    lru_l = rl.astype(hl.dtype) * jax.nn.gelu(ly_l)
    out_l = jnp.concatenate([_merge(att_l), lru_l], axis=-1) @ w_out
    out_c = None
    if need_ctx:
        att_c = _blocked_attention(_mla_q(cq_c, gq, wuq, None), k_c, v_c, scale)
        lru_c = rc.astype(hc.dtype) * jax.nn.gelu(ly_c)
        out_c = jnp.concatenate([_merge(att_c), lru_c], axis=-1) @ w_out
    return out_c, out_l


def _swiglu(h, wg, wu, wd):
    return (jax.nn.silu(h @ wg) * (h @ wu)) @ wd


def _moe(h, router, wg, wu, wd):
    logits = (h @ router).astype(jnp.float32)
    top_v, top_i = lax.top_k(logits, TOP_K)
    gates = jax.nn.softmax(top_v, axis=-1)
    combine = jnp.einsum('bnk,bnke->bne', gates, jax.nn.one_hot(top_i, N_EXPERTS, dtype=jnp.float32)).astype(h.dtype)
    out = jnp.zeros_like(h)
    for e in range(N_EXPERTS):
        out = out + combine[..., e:e + 1] * _swiglu(h, wg[e], wu[e], wd[e])
    return out


def setup_inputs(seed: int = 0) -> dict:
    key = jax.random.key(seed)
    ks = iter(jax.random.split(key, 32))
    f32 = jnp.float32

    def nrm(shape, scale):
        return jax.random.normal(next(ks), shape, f32) * scale

    D = D_MODEL
    ne, no = N_EVEN, N_ODD
    x = nrm((BATCH, SEQ, D), 1.0)
    c = nrm((BATCH, D), 1.0)
    ctx = nrm((BATCH, CTX_LEN, D), 1.0)
    c_ctx = nrm((D,), 1.0)
    w_mod = nrm((DEPTH, D, 6 * D), 0.5 * D ** -0.5)
    b_mod = nrm((DEPTH, 6 * D), 0.02)
    norm_g = 1.0 + nrm((DEPTH, 4, D), 0.05)
    w_in_even = nrm((ne, D, EVEN_IN), D ** -0.5)
    w_out_even = nrm((ne, EVEN_OUT, D), EVEN_OUT ** -0.5)
    m = jnp.arange(RET_HEADS, dtype=f32) + 5.0
    ret_base = jnp.log1p(-(2.0 ** -m)) + m * jnp.log(2.0)
    ret_decay = ret_base + nrm((ne, 2, RET_HEADS), 0.1)
    na_rpb = nrm((ne, NA_HEADS, 2 * NA_KH - 1, 2 * NA_KW - 1), 0.1)
    ffn_w_gate = nrm((ne, D, FFN_DENSE), D ** -0.5)
    ffn_w_up = nrm((ne, D, FFN_DENSE), D ** -0.5)
    ffn_w_down = nrm((ne, FFN_DENSE, D), FFN_DENSE ** -0.5)
    w_in_odd = nrm((no, D, ODD_IN), D ** -0.5)
    mla_q_norm = 1.0 + nrm((no, MLA_Q_RANK), 0.05)
    mla_w_uq = nrm((no, MLA_Q_RANK, MLA_HEADS * (MLA_NOPE + MLA_ROPE)), MLA_Q_RANK ** -0.5)
    mla_kv_norm = 1.0 + nrm((no, MLA_KV_RANK), 0.05)
    mla_w_ukv = nrm((no, MLA_KV_RANK, MLA_HEADS * (MLA_NOPE + MLA_V)), MLA_KV_RANK ** -0.5)
    lru_conv_w = nrm((no, LRU_CONV, LRU_WIDTH), LRU_CONV ** -0.5)
    lru_conv_b = nrm((no, LRU_WIDTH), 0.02)
    lru_w_a = nrm((no, 2, LRU_BLOCKS, LRU_BD, LRU_BD), LRU_BD ** -0.5)
    lru_b_a = nrm((no, 2, LRU_WIDTH), 0.02)
    lru_w_i = nrm((no, 2, LRU_BLOCKS, LRU_BD, LRU_BD), LRU_BD ** -0.5)
    lru_b_i = nrm((no, 2, LRU_WIDTH), 0.02)
    u = jax.random.uniform(next(ks), (no, 2, LRU_WIDTH), f32, 0.9, 0.999)
    sig = u ** (1.0 / LRU_C)
    lru_lambda = jnp.log(sig) - jnp.log1p(-sig)
    w_out_odd = nrm((no, ODD_OUT, D), ODD_OUT ** -0.5)
    router_w = nrm((no, D, N_EXPERTS), D ** -0.5)
    moe_w_gate = nrm((no, N_EXPERTS, D, FFN_EXPERT), D ** -0.5)
    moe_w_up = nrm((no, N_EXPERTS, D, FFN_EXPERT), D ** -0.5)
    moe_w_down = nrm((no, N_EXPERTS, FFN_EXPERT, D), FFN_EXPERT ** -0.5)
    return {'x': x, 'c': c, 'ctx': ctx, 'c_ctx': c_ctx, 'w_mod': w_mod, 'b_mod': b_mod, 'norm_g': norm_g,
            'w_in_even': w_in_even, 'w_out_even': w_out_even, 'ret_decay': ret_decay, 'na_rpb': na_rpb,
            'ffn_w_gate': ffn_w_gate, 'ffn_w_up': ffn_w_up, 'ffn_w_down': ffn_w_down,
            'w_in_odd': w_in_odd, 'mla_q_norm': mla_q_norm, 'mla_w_uq': mla_w_uq, 'mla_kv_norm': mla_kv_norm,
            'mla_w_ukv': mla_w_ukv, 'lru_conv_w': lru_conv_w, 'lru_conv_b': lru_conv_b, 'lru_w_a': lru_w_a,
            'lru_b_a': lru_b_a, 'lru_w_i': lru_w_i, 'lru_b_i': lru_b_i, 'lru_lambda': lru_lambda,
            'w_out_odd': w_out_odd, 'router_w': router_w, 'moe_w_gate': moe_w_gate, 'moe_w_up': moe_w_up,
            'moe_w_down': moe_w_down}


def reference(x, c, ctx, c_ctx, w_mod, b_mod, norm_g, w_in_even, w_out_even, ret_decay, na_rpb,
              ffn_w_gate, ffn_w_up, ffn_w_down, w_in_odd, mla_q_norm, mla_w_uq, mla_kv_norm, mla_w_ukv,
              lru_conv_w, lru_conv_b, lru_w_a, lru_b_a, lru_w_i, lru_b_i, lru_lambda, w_out_odd,
              router_w, moe_w_gate, moe_w_up, moe_w_down):
    b, s, d = x.shape
    rope_ret = _axial_rope(s, RET_DK)
    rope_mla = _axial_rope(s, MLA_ROPE)
    sc = jax.nn.silu(c)
    scc = jax.nn.silu(c_ctx)
    xl, xc = x, ctx
    for i in range(DEPTH):
        need_ctx = i < DEPTH - 1
        j = i // 2
        ml = (sc @ w_mod[i] + b_mod[i]).reshape(b, 6, 1, d)
        mc = (scc @ w_mod[i] + b_mod[i]).reshape(6, d)
        g = norm_g[i]
        hl = _rms_norm(xl, g[0]) * (1.0 + ml[:, 0]) + ml[:, 1]
        hc = _rms_norm(xc, g[0]) * (1.0 + mc[0]) + mc[1]
        if i % 2 == 0:
            oc, ol = _even_mixer(hc, hl, w_in_even[j], w_out_even[j], ret_decay[j], na_rpb[j], rope_ret, need_ctx)
            ffn = lambda h: _swiglu(h, ffn_w_gate[j], ffn_w_up[j], ffn_w_down[j])
        else:
            oc, ol = _odd_mixer(hc, hl, w_in_odd[j], mla_q_norm[j], mla_w_uq[j], mla_kv_norm[j], mla_w_ukv[j],
                                lru_conv_w[j], lru_conv_b[j], lru_w_a[j], lru_b_a[j], lru_w_i[j], lru_b_i[j],
                                lru_lambda[j], w_out_odd[j], rope_mla, need_ctx)
            ffn = lambda h: _moe(h, router_w[j], moe_w_gate[j], moe_w_up[j], moe_w_down[j])
        xl = xl + ml[:, 2] * _rms_norm(ol, g[1])
        hl = _rms_norm(xl, g[2]) * (1.0 + ml[:, 3]) + ml[:, 4]
        xl = xl + ml[:, 5] * _rms_norm(ffn(hl), g[3])
        if need_ctx:
            xc = xc + mc[2] * _rms_norm(oc, g[1])
            hc = _rms_norm(xc, g[2]) * (1.0 + mc[3]) + mc[4]
            xc = xc + mc[5] * _rms_norm(ffn(hc), g[3])
    return xl
```

```python
import functools

import numpy as np
import jax
import jax.numpy as jnp
from jax import lax
from jax.experimental import pallas as pl
from jax.experimental.pallas import tpu as pltpu

F32 = jnp.float32
BF16 = jnp.bfloat16

EPS = 1e-6
NEG_INF = -1e30
ROPE_BASE = 10000.0
GRID_W = 64
RET_HEADS = 8
RET_CHUNK = 128
NA_HEADS = 16
NA_KH = 8
NA_KW = 16
NA_QROWS = 8
MLA_HEADS = 16
MLA_Q_RANK = 1536
MLA_KV_RANK = 512
MLA_NOPE = 128
MLA_ROPE = 64
MLA_V = 128
LRU_C = 8.0
LRU_CONV = 4
TOP_K = 2

LANE = 128
SUBLANE = 8
VMEM_LIMIT = 52 * 1024 * 1024


def _cparams(sem):
    return pltpu.CompilerParams(dimension_semantics=sem, vmem_limit_bytes=VMEM_LIMIT)


def _tile(n, target, mult=LANE):
    if n <= target:
        return n
    best = None
    for t in range(mult, target + 1, mult):
        if n % t == 0:
            best = t
    assert best is not None, (n, target, mult)
    return best


def _dot(a, b):
    return jnp.dot(a, b, preferred_element_type=F32)


def _dot_nt(a, b):
    return lax.dot_general(a, b, (((1,), (1,)), ((), ())), preferred_element_type=F32)


def _dot_tn(a, b):
    return lax.dot_general(a, b, (((0,), (0,)), ((), ())), preferred_element_type=F32)


def _rms(x, g):
    return x * lax.rsqrt(jnp.mean(x * x, axis=-1, keepdims=True) + EPS) * g


def _mod_kernel(s_ref, w_ref, b_ref, o_ref):
    s = s_ref[...]
    s = (s * jax.nn.sigmoid(s)).astype(BF16)
    o_ref[0] = _dot(s, w_ref[0].astype(BF16)) + b_ref[0]


def _modulation(s_in, w_mod, b_mod):
    nl, d, n = w_mod.shape
    rows = s_in.shape[0]
    tn = _tile(n, 512)
    return pl.pallas_call(
        _mod_kernel,
        out_shape=jax.ShapeDtypeStruct((nl, rows, n), F32),
        grid=(nl, n // tn),
        in_specs=[pl.BlockSpec((rows, d), lambda l, j: (0, 0)),
                  pl.BlockSpec((1, d, tn), lambda l, j: (l, 0, j)),
                  pl.BlockSpec((1, 1, tn), lambda l, j: (l, 0, j))],
        out_specs=pl.BlockSpec((1, rows, tn), lambda l, j: (l, 0, j)),
        compiler_params=_cparams(("arbitrary", "arbitrary")),
        name="modulation",
    )(s_in, w_mod, b_mod.reshape(nl, 1, n))


def _resnorm_kernel(*refs, has_y, has_next):
    it = iter(refs)
    x_ref = next(it)
    if has_y:
        y_ref, gate_ref, gy_ref = next(it), next(it), next(it)
    if has_next:
        gn_ref, sc_ref, sh_ref = next(it), next(it), next(it)
    x = x_ref[...]
    if has_y:
        xo_ref = next(it)
        x = x + gate_ref[0] * _rms(y_ref[...], gy_ref[...])
        xo_ref[...] = x
    if has_next:
        h_ref = next(it)
        h_ref[...] = (_rms(x, gn_ref[...]) * (1.0 + sc_ref[0]) + sh_ref[0]).astype(BF16)


def _resnorm(x, nrows, mod_row, *, y=None, gate=None, gy=None, gn=None, sc=None, sh=None, tr=256):
    d = x.shape[1]
    tr = _tile(nrows, tr, SUBLANE)
    has_y, has_next = y is not None, gn is not None
    row = pl.BlockSpec((tr, d), lambda m: (m, 0))
    vec = pl.BlockSpec((1, d), lambda m: (0, 0))
    modv = pl.BlockSpec((1, 1, d), lambda m: (mod_row(m * tr), 0, 0))
    args, specs, outs, ospecs = [x], [row], [], []
    if has_y:
        args += [y, gate, gy.reshape(1, d)]
        specs += [row, modv, vec]
        outs.append(jax.ShapeDtypeStruct((nrows, d), F32))
        ospecs.append(row)
    if has_next:
        args += [gn.reshape(1, d), sc, sh]
        specs += [vec, modv, modv]
        outs.append(jax.ShapeDtypeStruct((nrows, d), BF16))
        ospecs.append(row)
    res = pl.pallas_call(
        functools.partial(_resnorm_kernel, has_y=has_y, has_next=has_next),
        out_shape=outs, grid=(nrows // tr,), in_specs=specs, out_specs=ospecs,
        compiler_params=_cparams(("parallel",)),
        name="resnorm",
    )(*args)
    return res


def _mm_kernel(*refs, nparts, has_norm):
    x_refs, w_refs = refs[:nparts], refs[nparts:2 * nparts]
    pos = 2 * nparts
    if has_norm:
        g_ref, o_ref, h_ref = refs[pos], refs[pos + 1], refs[pos + 2]

        @pl.when(pl.program_id(1) == 0)
        def _():
            h_ref[...] = _rms(x_refs[0][...].astype(F32), g_ref[...]).astype(BF16)

        acc = _dot(h_ref[...], w_refs[0][...])
    else:
        o_ref = refs[pos]
        acc = _dot(x_refs[0][...], w_refs[0][...])
        for i in range(1, nparts):
            acc = acc + _dot(x_refs[i][...], w_refs[i][...])
    o_ref[...] = acc.astype(o_ref.dtype)


def _mm(xs, ws, m, n, out_dtype, *, norm_g=None, tm=1024, tn=512, name="mm"):
    tm, tn = _tile(m, tm, SUBLANE), _tile(n, tn)
    has_norm = norm_g is not None
    args, specs = [], []
    for (a, k, cb) in xs:
        args.append(a)
        specs.append(pl.BlockSpec((tm, k), lambda i, j, cb=cb: (i, cb)))
    for (w, k, rb) in ws:
        args.append(w)
        specs.append(pl.BlockSpec((k, tn), lambda i, j, rb=rb: (rb, j)))
    scratch = []
    if has_norm:
        k0 = xs[0][1]
        args.append(norm_g.reshape(1, k0).astype(F32))
        specs.append(pl.BlockSpec((1, k0), lambda i, j: (0, 0)))
        scratch.append(pltpu.VMEM((tm, k0), BF16))
    return pl.pallas_call(
        functools.partial(_mm_kernel, nparts=len(xs), has_norm=has_norm),
        out_shape=jax.ShapeDtypeStruct((m, n), out_dtype),
        grid=(m // tm, n // tn), in_specs=specs,
        out_specs=pl.BlockSpec((tm, tn), lambda i, j: (i, j)),
        scratch_shapes=scratch,
        compiler_params=_cparams(("parallel", "arbitrary")),
        name=name,
    )(*args)


def _ffn_kernel(h_ref, wg_ref, wu_ref, wd_ref, o_ref):
    f = pl.program_id(1)
    h = h_ref[...]
    g = _dot(h, wg_ref[...])
    u = _dot(h, wu_ref[...])
    a = (g * jax.nn.sigmoid(g) * u).astype(BF16)
    y = _dot(a, wd_ref[...])

    @pl.when(f == 0)
    def _():
        o_ref[...] = y

    @pl.when(f > 0)
    def _():
        o_ref[...] += y


def _ffn(h, wg, wu, wd, *, tm=512, tf=256):
    m, d = h.shape
    f = wg.shape[1]
    tm, tf = _tile(m, tm, SUBLANE), _tile(f, tf)
    return pl.pallas_call(
        _ffn_kernel,
        out_shape=jax.ShapeDtypeStruct((m, d), F32),
        grid=(m // tm, f // tf),
        in_specs=[pl.BlockSpec((tm, d), lambda i, j: (i, 0)),
                  pl.BlockSpec((d, tf), lambda i, j: (0, j)),
                  pl.BlockSpec((d, tf), lambda i, j: (0, j)),
                  pl.BlockSpec((tf, d), lambda i, j: (j, 0))],
        out_specs=pl.BlockSpec((tm, d), lambda i, j: (i, 0)),
        compiler_params=_cparams(("parallel", "arbitrary")),
        name="ffn",
    )(h, wg, wu, wd)


def _router_kernel(h_ref, r_ref, o_ref, *, n_exp):
    logits = jnp.dot(h_ref[...].astype(F32), r_ref[...], preferred_element_type=F32,
                     precision=lax.Precision.HIGHEST)
    lane = lax.broadcasted_iota(jnp.int32, logits.shape, 1)
    big = jnp.int32(2 ** 30)
    logits = jnp.where(lane < n_exp, logits, -jnp.inf)
    m1 = jnp.max(logits, axis=-1, keepdims=True)
    i1 = jnp.min(jnp.where(logits == m1, lane, big), axis=-1, keepdims=True)
    rest = jnp.where(lane == i1, -jnp.inf, logits)
    m2 = jnp.max(rest, axis=-1, keepdims=True)
    i2 = jnp.min(jnp.where(rest == m2, lane, big), axis=-1, keepdims=True)
    e2 = jnp.exp(m2 - m1)
    g1 = 1.0 / (1.0 + e2)
    g2 = e2 / (1.0 + e2)
    o_ref[...] = jnp.where(lane == i1, g1, 0.0) + jnp.where(lane == i2, g2, 0.0)


def _router(h, router_w):
    m, d = h.shape
    n_exp = router_w.shape[1]
    assert n_exp <= LANE and TOP_K == 2
    rw = jnp.pad(router_w.astype(F32), ((0, 0), (0, LANE - n_exp)))
    tm = _tile(m, 512, SUBLANE)
    return pl.pallas_call(
        functools.partial(_router_kernel, n_exp=n_exp),
        out_shape=jax.ShapeDtypeStruct((m, LANE), F32),
        grid=(m // tm,),
        in_specs=[pl.BlockSpec((tm, d), lambda i: (i, 0)),
                  pl.BlockSpec((d, LANE), lambda i: (0, 0))],
        out_specs=pl.BlockSpec((tm, LANE), lambda i: (i, 0)),
        compiler_params=_cparams(("parallel",)),
        name="router",
    )(h, rw)


def _moe_kernel(h_ref, c_ref, wg_ref, wu_ref, wd_ref, o_ref):
    e, f = pl.program_id(1), pl.program_id(2)
    h = h_ref[...]
    g = _dot(h, wg_ref[0])
    u = _dot(h, wu_ref[0])
    comb = c_ref[...]
    lane = lax.broadcasted_iota(jnp.int32, comb.shape, 1)
    cw = jnp.sum(jnp.where(lane == e, comb, 0.0), axis=-1, keepdims=True)
    a = (g * jax.nn.sigmoid(g) * u).astype(BF16)
    y = cw * _dot(a, wd_ref[0])
    first = jnp.logical_and(e == 0, f == 0)

    @pl.when(first)
    def _():
        o_ref[...] = y

    @pl.when(jnp.logical_not(first))
    def _():
        o_ref[...] += y


def _moe(h, comb, wg, wu, wd, *, tm=512, tf=256):
    m, d = h.shape
    n_exp, _, f = wg.shape
    tm, tf = _tile(m, tm, SUBLANE), _tile(f, tf)
    return pl.pallas_call(
        _moe_kernel,
        out_shape=jax.ShapeDtypeStruct((m, d), F32),
        grid=(m // tm, n_exp, f // tf),
        in_specs=[pl.BlockSpec((tm, d), lambda i, e, j: (i, 0)),
                  pl.BlockSpec((tm, LANE), lambda i, e, j: (i, 0)),
                  pl.BlockSpec((1, d, tf), lambda i, e, j: (e, 0, j)),
                  pl.BlockSpec((1, d, tf), lambda i, e, j: (e, 0, j)),
                  pl.BlockSpec((1, tf, d), lambda i, e, j: (e, j, 0))],
        out_specs=pl.BlockSpec((tm, d), lambda i, e, j: (i, 0)),
        compiler_params=_cparams(("parallel", "arbitrary", "arbitrary")),
        name="moe",
    )(h, comb, wg, wu, wd)


def _ret_kernel(dec_ref, ql_ref, kl_ref, vl_ref, gl_ref, qc_ref, kc_ref, vc_ref, gc_ref, ca_ref, sa_ref,
                ol_ref, oc_ref, qs, ks, kcs, af_l, ab_l, af_c, ab_c, st_f, st_b, *, chunk, kscale):
    hd = pl.program_id(1)
    s_len, dk = ql_ref.shape
    c_len = qc_ref.shape[0]
    dv = vl_ref.shape[1]
    ch = chunk
    ncl, ncc = s_len // ch, c_len // ch
    half = LANE // 2

    def sl(c):
        return pl.ds(pl.multiple_of(c * ch, ch), ch)

    lane = lax.broadcasted_iota(jnp.int32, (ch, LANE), 1)
    lo = lane < half
    sign = jnp.where(lo, -1.0, 1.0)

    def prep(c, _):
        r = sl(c)
        ca, sa = ca_ref[r, :], sa_ref[r, :]
        car, sar = pltpu.roll(ca, half, 1), pltpu.roll(sa, half, 1)
        cos = (jnp.where(lo, ca, car), jnp.where(lo, car, ca))
        sin = (sign * jnp.where(lo, sa, sar), sign * jnp.where(lo, sar, sa))

        def rope(x):
            parts = []
            for p in range(dk // LANE):
                xp = x[:, p * LANE:(p + 1) * LANE]
                parts.append(xp * cos[p] + pltpu.roll(xp, half, 1) * sin[p])
            return jnp.concatenate(parts, axis=-1)

        qs[r, :] = rope(ql_ref[r, :].astype(F32)).astype(BF16)
        ks[r, :] = (rope(kl_ref[r, :].astype(F32)) * kscale).astype(BF16)
        return 0

    lax.fori_loop(0, ncl, prep, 0)
    kcs[...] = (kc_ref[...].astype(F32) * kscale).astype(BF16)

    def log_gamma(dirn):
        d = jnp.full((1, 1), dec_ref[dirn, hd], F32)
        return -(jnp.maximum(-d, 0.0) + jnp.log1p(jnp.exp(-jnp.abs(d))))

    ii = lax.broadcasted_iota(jnp.int32, (ch, ch), 0).astype(F32)
    jj = lax.broadcasted_iota(jnp.int32, (ch, ch), 1).astype(F32)
    pos_k = lax.broadcasted_iota(jnp.int32, (ch, dk), 0).astype(F32)
    lg_f, lg_b = log_gamma(0), log_gamma(1)
    intra_f = jnp.where(ii >= jj, jnp.exp(lg_f * jnp.maximum(ii - jj, 0.0)), 0.0)
    intra_b = jnp.where(jj >= ii, jnp.exp(lg_b * jnp.maximum(jj - ii, 0.0)), 0.0)
    qdec_f = jnp.exp(lg_f * (pos_k + 1.0))
    kdec_f = jnp.exp(lg_f * (ch - 1.0 - pos_k))
    qdec_b = jnp.exp(lg_b * (ch - pos_k))
    kdec_b = jnp.exp(lg_b * pos_k)
    sdec_f = jnp.exp(lg_f * ch)
    sdec_b = jnp.exp(lg_b * ch)

    def step(q_src, k_src, v_src, acc, st, c, intra, qdec, kdec, sdec):
        r = sl(c)
        q, k, v = q_src[r, :], k_src[r, :], v_src[r, :]
        att = _dot_nt(q, k) * intra
        state = st[...]
        out = _dot(att.astype(BF16), v) + _dot((q.astype(F32) * qdec).astype(BF16), state.astype(BF16))
        acc[r, :] = out
        st[...] = state * sdec + _dot_tn((k.astype(F32) * kdec).astype(BF16), v)

    st_f[...] = jnp.zeros_like(st_f)
    st_b[...] = jnp.zeros_like(st_b)

    def scan(q_src, k_src, v_src, acc_f, acc_b, n):
        def body(c, _):
            step(q_src, k_src, v_src, acc_f, st_f, c, intra_f, qdec_f, kdec_f, sdec_f)
            step(q_src, k_src, v_src, acc_b, st_b, n - 1 - c, intra_b, qdec_b, kdec_b, sdec_b)
            return 0
        lax.fori_loop(0, n, body, 0)

    scan(qc_ref, kcs, vc_ref, af_c, ab_c, ncc)
    scan(qs, ks, vl_ref, af_l, ab_l, ncl)

    def finish(acc_f, acc_b, g_ref, o_ref, n):
        def body(c, _):
            r = sl(c)
            y = acc_f[r, :] + acc_b[r, :]
            y = y * lax.rsqrt(jnp.mean(y * y, axis=-1, keepdims=True) + EPS)
            g = g_ref[r, :].astype(F32)
            o_ref[r, :] = (y * (g * jax.nn.sigmoid(g))).astype(o_ref.dtype)
            return 0
        lax.fori_loop(0, n, body, 0)

    finish(af_l, ab_l, gl_ref, ol_ref, ncl)
    finish(af_c, ab_c, gc_ref, oc_ref, ncc)


def _retention(p0, ret_decay, rope_a, rope_b, b, s, c_len, dk, dv):
    nh = RET_HEADS
    assert dk == 2 * LANE and dv % LANE == 0 and s % RET_CHUNK == 0 and c_len % RET_CHUNK == 0
    assert s % c_len == 0
    cb0 = (b * s) // c_len
    assert dk == dv
    qo, ko, vo, go = 0, nh, 2 * nh, 3 * nh
    lat = lambda off: pl.BlockSpec((s, dk), lambda i, h, off=off: (i, off + h))
    ctx = lambda off: pl.BlockSpec((c_len, dk), lambda i, h, off=off: (cb0 + i, off + h))
    tab = pl.BlockSpec((s, LANE), lambda i, h: (0, 0))
    return pl.pallas_call(
        functools.partial(_ret_kernel, chunk=RET_CHUNK, kscale=dk ** -0.5),
        out_shape=[jax.ShapeDtypeStruct((b * s, nh * dv), BF16),
                   jax.ShapeDtypeStruct((b * c_len, nh * dv), BF16)],
        grid=(b, nh),
        in_specs=[pl.BlockSpec(memory_space=pltpu.SMEM),
                  lat(qo), lat(ko), lat(vo), lat(go), ctx(qo), ctx(ko), ctx(vo), ctx(go), tab, tab],
        out_specs=[pl.BlockSpec((s, dv), lambda i, h: (i, h)),
                   pl.BlockSpec((c_len, dv), lambda i, h: (i, h))],
        scratch_shapes=[pltpu.VMEM((s, dk), BF16), pltpu.VMEM((s, dk), BF16), pltpu.VMEM((c_len, dk), BF16),
                        pltpu.VMEM((s, dv), F32), pltpu.VMEM((s, dv), F32),
                        pltpu.VMEM((c_len, dv), F32), pltpu.VMEM((c_len, dv), F32),
                        pltpu.VMEM((dk, dv), F32), pltpu.VMEM((dk, dv), F32)],
        compiler_params=_cparams(("parallel", "arbitrary")),
        name="retention",
    )(ret_decay.astype(F32), p0, p0, p0, p0, p0, p0, p0, p0, rope_a, rope_b)


def _na_kernel(q_ref, k_ref, v_ref, qc_ref, kc_ref, vc_ref, bias_ref, o_ref, oc_ref, *, qrows, krows, rows, scale):
    blk = pl.program_id(2)
    w = q_ref.shape[0] // qrows
    k0 = jnp.clip(blk * qrows - NA_KH // 2, 0, rows - krows)
    win = pl.ds(pl.multiple_of(k0 * w, w), krows * w)
    q, kc, vc = q_ref[...], kc_ref[...], vc_ref[...]
    s_loc = _dot_nt(q, k_ref[win, :]) * scale + bias_ref[0, 0]
    s_ctx = _dot_nt(q, kc) * scale
    m = jnp.maximum(jnp.max(s_loc, axis=-1, keepdims=True), jnp.max(s_ctx, axis=-1, keepdims=True))
    p_loc, p_ctx = jnp.exp(s_loc - m), jnp.exp(s_ctx - m)
    l = jnp.sum(p_loc, axis=-1, keepdims=True) + jnp.sum(p_ctx, axis=-1, keepdims=True)
    o = _dot(p_loc.astype(BF16), v_ref[win, :]) + _dot(p_ctx.astype(BF16), vc)
    o_ref[...] = (o / l).astype(o_ref.dtype)

    @pl.when(blk == 0)
    def _():
        sc = _dot_nt(qc_ref[...], kc) * scale
        pc = jnp.exp(sc - jnp.max(sc, axis=-1, keepdims=True))
        oc = _dot(pc.astype(BF16), vc) / jnp.sum(pc, axis=-1, keepdims=True)
        oc_ref[...] = oc.astype(oc_ref.dtype)


def _na_bias(rpb, rows):
    w, kh, kw, qr = GRID_W, NA_KH, NA_KW, NA_QROWS
    kr = qr + kh - 1
    qc = np.arange(w)[:, None]
    kc = np.arange(w)[None, :]
    cstart = np.clip(qc - kw // 2, 0, w - kw)
    col_ok = (kc >= cstart) & (kc < cstart + kw)
    dc = np.clip(kc - qc + kw - 1, 0, 2 * kw - 2)
    per_dr = jnp.where(col_ok[None, None], rpb[:, :, dc], NEG_INF)
    r0s = np.array([0, qr, rows - qr])
    k0s = np.clip(r0s - kh // 2, 0, rows - kr)
    qrow = r0s[:, None, None] + np.arange(qr)[None, :, None]
    krow = k0s[:, None, None] + np.arange(kr)[None, None, :]
    win0 = np.clip(qrow - kh // 2, 0, rows - kh)
    row_ok = (krow >= win0) & (krow < win0 + kh)
    dr = np.clip(krow - qrow + kh - 1, 0, 2 * kh - 2)
    full = per_dr[:, dr]
    full = jnp.where(row_ok[None, :, :, :, None, None], full, NEG_INF)
    full = jnp.transpose(full, (0, 1, 2, 4, 3, 5))
    return full.reshape(rpb.shape[0], 3, qr * w, kr * w).astype(F32)


def _neighbourhood(p0, rpb, b, s, c_len, hd, col0):
    nh, w, qr = NA_HEADS, GRID_W, NA_QROWS
    rows = s // w
    kr = qr + NA_KH - 1
    assert rows >= kr and rows % qr == 0 and hd == LANE and s % c_len == 0
    nblk = rows // qr
    cb0 = (b * s) // c_len
    qo, ko, vo = col0, col0 + nh, col0 + 2 * nh
    bias = _na_bias(rpb, rows)
    pat = lambda k: jnp.where(k == 0, 0, jnp.where(k == nblk - 1, 2, 1))
    lat = lambda off: pl.BlockSpec((s, hd), lambda i, h, k, off=off: (i, off + h))
    ctx = lambda off: pl.BlockSpec((c_len, hd), lambda i, h, k, off=off: (cb0 + i, off + h))
    return pl.pallas_call(
        functools.partial(_na_kernel, qrows=qr, krows=kr, rows=rows, scale=hd ** -0.5),
        out_shape=[jax.ShapeDtypeStruct((b * s, nh * hd), BF16),
                   jax.ShapeDtypeStruct((b * c_len, nh * hd), BF16)],
        grid=(b, nh, nblk),
        in_specs=[pl.BlockSpec((qr * w, hd), lambda i, h, k: (i * nblk + k, qo + h)),
                  lat(ko), lat(vo), ctx(qo), ctx(ko), ctx(vo),
                  pl.BlockSpec((1, 1, qr * w, kr * w), lambda i, h, k: (h, pat(k), 0, 0))],
        out_specs=[pl.BlockSpec((qr * w, hd), lambda i, h, k: (i * nblk + k, h)),
                   pl.BlockSpec((c_len, hd), lambda i, h, k: (i, h))],
        compiler_params=_cparams(("parallel", "parallel", "arbitrary")),
        name="neighbourhood",
    )(p0, p0, p0, p0, p0, p0, bias)


def _kpe_kernel(x_ref, t_ref, o_ref, *, n_lat):
    x = x_ref[...].astype(F32)
    half = LANE // 2
    lane = lax.broadcasted_iota(jnp.int32, x.shape, 1)
    r = x * t_ref[...]
    roped = r + pltpu.roll(r, half, 1)
    is_lat = pl.program_id(0) < n_lat
    o_ref[...] = jnp.where(lane < half, jnp.where(is_lat, roped, x), 0.0).astype(o_ref.dtype)


def _mla_kernel(qn_ref, qp_ref, tq_ref, kn_ref, kp_ref, v_ref, knc_ref, kpc_ref, vc_ref, o_ref,
                q_s, m_s, l_s, acc_s, *, scale):
    ki = pl.program_id(3)
    half = LANE // 2

    def update(k, v, first):
        s = _dot_nt(q_s[...], k) * scale
        m_cur = jnp.max(s, axis=-1, keepdims=True)
        if first:
            m_new = m_cur
            p = jnp.exp(s - m_new)
            l_s[...] = jnp.sum(p, axis=-1, keepdims=True)
            acc_s[...] = _dot(p.astype(BF16), v)
        else:
            m_old = m_s[...]
            m_new = jnp.maximum(m_old, m_cur)
            alpha = jnp.exp(m_old - m_new)
            p = jnp.exp(s - m_new)
            l_s[...] = alpha * l_s[...] + jnp.sum(p, axis=-1, keepdims=True)
            acc_s[...] = alpha * acc_s[...] + _dot(p.astype(BF16), v)
        m_s[...] = m_new

    @pl.when(ki == 0)
    def _():
        r = qp_ref[...].astype(F32) * tq_ref[...]
        qpe = r + pltpu.roll(r, half, 1)
        q_s[...] = jnp.concatenate([qn_ref[...], qpe.astype(BF16)], axis=-1)
        update(jnp.concatenate([knc_ref[...], kpc_ref[...]], axis=-1), vc_ref[...], True)

    update(jnp.concatenate([kn_ref[...], kp_ref[...]], axis=-1), v_ref[...], False)

    @pl.when(ki == pl.num_programs(3) - 1)
    def _():
        o_ref[...] = (acc_s[...] / l_s[...]).astype(o_ref.dtype)


def _mla_attention(q, kv, kpe, tab, b, s, c_len, *, tq=512, tk=512):
    nh = MLA_HEADS
    assert MLA_NOPE == LANE and MLA_V == LANE and 2 * MLA_ROPE == LANE
    tq, tk = _tile(s, tq, SUBLANE), _tile(s, tk, SUBLANE)
    nq, nk = s // tq, s // tk
    cb0 = (b * s) // c_len
    return pl.pallas_call(
        functools.partial(_mla_kernel, scale=(MLA_NOPE + MLA_ROPE) ** -0.5),
        out_shape=jax.ShapeDtypeStruct((b * s, nh * MLA_V), BF16),
        grid=(b, nh, nq, nk),
        in_specs=[pl.BlockSpec((tq, LANE), lambda i, h, a, k: (i * nq + a, h)),
                  pl.BlockSpec((tq, LANE), lambda i, h, a, k: (i * nq + a, nh + h)),
                  pl.BlockSpec((tq, LANE), lambda i, h, a, k: (a, 0)),
                  pl.BlockSpec((tk, LANE), lambda i, h, a, k: (i * nk + k, h)),
                  pl.BlockSpec((tk, LANE), lambda i, h, a, k: (i * nk + k, 0)),
                  pl.BlockSpec((tk, LANE), lambda i, h, a, k: (i * nk + k, nh + h)),
                  pl.BlockSpec((c_len, LANE), lambda i, h, a, k: (cb0 + i, h)),
                  pl.BlockSpec((c_len, LANE), lambda i, h, a, k: (cb0 + i, 0)),
                  pl.BlockSpec((c_len, LANE), lambda i, h, a, k: (cb0 + i, nh + h))],
        out_specs=pl.BlockSpec((tq, MLA_V), lambda i, h, a, k: (i * nq + a, h)),
        scratch_shapes=[pltpu.VMEM((tq, 2 * LANE), BF16), pltpu.VMEM((tq, 1), F32),
                        pltpu.VMEM((tq, 1), F32), pltpu.VMEM((tq, MLA_V), F32)],
        compiler_params=_cparams(("parallel", "parallel", "parallel", "arbitrary")),
        name="mla_attention",
    )(q, q, tab, kv, kpe, kv, kv, kpe, kv)


def _lru_kernel(xl_ref, xc_ref, yl_ref, cw_ref, cb_ref, wa_ref, wi_ref, ba_ref, bi_ref, lam_ref, o_ref,
                xp_l, xp_c, a_f, u_f, a_b, u_b, *, rows_per_step):
    s_len, c_len = xl_ref.shape[0], xc_ref.shape[0]
    pad = SUBLANE
    left = (LRU_CONV - 1) // 2
    cw, cb = cw_ref[...], cb_ref[...]

    def softplus(x):
        return jnp.maximum(x, 0.0) + jnp.log1p(jnp.exp(-jnp.abs(x)))

    sp = [softplus(-lam_ref[d:d + 1, :]) for d in range(2)]

    def gates(xp, n, f_off, b_off):
        step = min(rows_per_step, n)
        for r0 in range(0, n, step):
            x = cb
            for j in range(LRU_CONV):
                x = x + cw[j:j + 1, :] * xp[pad - left + j + r0: pad - left + j + r0 + step, :]
            xb = x.astype(BF16)
            for d, (a_ref, u_ref, off) in enumerate(((a_f, u_f, f_off), (a_b, u_b, b_off))):
                rg = jax.nn.sigmoid(_dot(xb, wa_ref[d, 0]) + ba_ref[d:d + 1, :])
                ig = jax.nn.sigmoid(_dot(xb, wi_ref[d, 0]) + bi_ref[d:d + 1, :])
                log_a = -LRU_C * rg * sp[d]
                a_ref[off + r0: off + r0 + step, :] = jnp.exp(log_a)
                th = jnp.tanh(log_a)
                u_ref[off + r0: off + r0 + step, :] = jnp.sqrt(-2.0 * th / (1.0 - th)) * ig * x

    zeros = jnp.zeros((pad, xl_ref.shape[1]), F32)
    for xp, src, n in ((xp_l, xl_ref, s_len), (xp_c, xc_ref, c_len)):
        xp[0:pad, :] = zeros
        xp[pad + n: 2 * pad + n, :] = zeros
        xp[pad: pad + n, :] = src[...].astype(F32)
    gates(xp_c, c_len, 0, s_len)
    gates(xp_l, s_len, c_len, 0)

    n_tiles = (s_len + c_len) // SUBLANE
    row = lax.broadcasted_iota(jnp.int32, (SUBLANE, xl_ref.shape[1]), 0)

    def body(i, carry):
        hf, hb = carry
        rf = pl.ds(pl.multiple_of(i * SUBLANE, SUBLANE), SUBLANE)
        a, u = a_f[rf, :], u_f[rf, :]
        for sft in (1, 2, 4):
            m = row >= sft
            u = jnp.where(m, a * pltpu.roll(u, sft, 0) + u, u)
            a = jnp.where(m, a * pltpu.roll(a, sft, 0), a)
        h = a * hf + u
        u_f[rf, :] = h
        hf = h[SUBLANE - 1:SUBLANE, :]
        rb = pl.ds(pl.multiple_of((n_tiles - 1 - i) * SUBLANE, SUBLANE), SUBLANE)
        a, u = a_b[rb, :], u_b[rb, :]
        for sft in (1, 2, 4):
            m = row < SUBLANE - sft
            u = jnp.where(m, a * pltpu.roll(u, SUBLANE - sft, 0) + u, u)
            a = jnp.where(m, a * pltpu.roll(a, SUBLANE - sft, 0), a)
        h = a * hb + u
        u_b[rb, :] = h
        hb = h[0:1, :]
        return hf, hb

    h0 = jnp.zeros((1, xl_ref.shape[1]), F32)
    lax.fori_loop(0, n_tiles, body, (h0, h0), unroll=4)

    step = min(rows_per_step, s_len)
    for r0 in range(0, s_len, step):
        y = yl_ref[r0:r0 + step, :].astype(F32)
        h = u_f[c_len + r0: c_len + r0 + step, :] + u_b[r0:r0 + step, :]
        o_ref[r0:r0 + step, :] = (h * jax.nn.gelu(y)).astype(o_ref.dtype)


def _rglru(p1, conv_w, conv_b, wa, wi, ba, bi, lam, b, s, c_len, x_col0, y_col0):
    nblk, bd = wa.shape[1], wa.shape[2]
    width = nblk * bd
    assert bd == LANE and s % c_len == 0 and (s + c_len) % (4 * SUBLANE) == 0
    cb0 = (b * s) // c_len
    t = s + c_len
    vec2 = pl.BlockSpec((2, bd), lambda i, k: (0, k))
    wspec = pl.BlockSpec((2, 1, bd, bd), lambda i, k: (0, k, 0, 0))
    return pl.pallas_call(
        functools.partial(_lru_kernel, rows_per_step=512),
        out_shape=jax.ShapeDtypeStruct((b * s, width), BF16),
        grid=(b, nblk),
        in_specs=[pl.BlockSpec((s, bd), lambda i, k: (i, x_col0 + k)),
                  pl.BlockSpec((c_len, bd), lambda i, k: (cb0 + i, x_col0 + k)),
                  pl.BlockSpec((s, bd), lambda i, k: (i, y_col0 + k)),
                  pl.BlockSpec((LRU_CONV, bd), lambda i, k: (0, k)),
                  pl.BlockSpec((1, bd), lambda i, k: (0, k)),
                  wspec, wspec, vec2, vec2, vec2],
        out_specs=pl.BlockSpec((s, bd), lambda i, k: (i, k)),
        scratch_shapes=[pltpu.VMEM((s + 2 * SUBLANE, bd), F32), pltpu.VMEM((c_len + 2 * SUBLANE, bd), F32),
                        pltpu.VMEM((t, bd), F32), pltpu.VMEM((t, bd), F32),
                        pltpu.VMEM((t, bd), F32), pltpu.VMEM((t, bd), F32)],
        compiler_params=_cparams(("parallel", "parallel")),
        name="rglru",
    )(p1, p1, p1, conv_w.astype(F32), conv_b.reshape(1, width).astype(F32),
      wa.astype(BF16), wi.astype(BF16), ba.astype(F32), bi.astype(F32), lam.astype(F32))


def _rope_angles(n, rot_dim):
    t = np.arange(n)
    row = (t // GRID_W).astype(np.float32)
    col = (t % GRID_W).astype(np.float32)
    nf = rot_dim // 4
    inv = jnp.asarray(ROPE_BASE, F32) ** (-jnp.arange(nf, dtype=F32) / nf)
    return row[:, None] * inv, col[:, None] * inv


def _swap_perm(rot_dim):
    q = rot_dim // 4
    idx = np.arange(rot_dim).reshape(4, q)
    return idx[[1, 0, 3, 2]].reshape(-1)


def kernel(x, c, ctx, c_ctx, w_mod, b_mod, norm_g, w_in_even, w_out_even, ret_decay, na_rpb, ffn_w_gate, ffn_w_up,
           ffn_w_down, w_in_odd, mla_q_norm, mla_w_uq, mla_kv_norm, mla_w_ukv, lru_conv_w, lru_conv_b, lru_w_a,
           lru_b_a, lru_w_i, lru_b_i, lru_lambda, w_out_odd, router_w, moe_w_gate, moe_w_up, moe_w_down):
    b, s, d = x.shape
    c_len = ctx.shape[1]
    n_lat, n_ctx = b * s, b * c_len
    n_tok = n_lat + n_ctx
    depth = w_mod.shape[0]
    assert depth == 2 and w_in_even.shape[0] == 1 and w_in_odd.shape[0] == 1
    assert n_ctx % SUBLANE == 0 and s % c_len == 0

    mod_rows = -(-(b + 1) // SUBLANE) * SUBLANE
    s_in = jnp.concatenate([c, c_ctx[None], jnp.zeros((mod_rows - b - 1, d), F32)], axis=0)
    mods = _modulation(s_in, w_mod, b_mod).reshape(depth, mod_rows, 6, 1, d)
    mod = lambda l, j: mods[l, :, j]
    mod_row = lambda r0: jnp.where(r0 < n_lat, r0 // s, b)

    xs = jnp.concatenate([x.reshape(n_lat, d), ctx.reshape(n_ctx, d)], axis=0)

    g = norm_g[0]
    (h,) = _resnorm(xs, n_tok, mod_row, gn=g[0], sc=mod(0, 0), sh=mod(0, 1))
    w_in = w_in_even[0].astype(BF16)
    p0 = _mm([(h, d, 0)], [(w_in, d, 0)], n_tok, w_in.shape[1], BF16, name="in_proj_even")
    ret_w = w_out_even.shape[1] - NA_HEADS * LANE
    dk = ret_w // RET_HEADS
    ar, ac = _rope_angles(s, dk)
    rope_a = jnp.concatenate([jnp.cos(ar), jnp.cos(ac)], axis=-1)
    rope_b = jnp.concatenate([jnp.sin(ar), jnp.sin(ac)], axis=-1)
    ret_l, ret_c = _retention(p0, ret_decay[0], rope_a, rope_b, b, s, c_len, dk, dk)
    na_l, na_c = _neighbourhood(p0, na_rpb[0], b, s, c_len, LANE, 4 * ret_w // LANE)
    mix_a = jnp.concatenate([ret_l, ret_c], axis=0)
    mix_b = jnp.concatenate([na_l, na_c], axis=0)
    w_out = w_out_even[0].astype(BF16)
    half_k = w_out.shape[0] // 2
    assert mix_a.shape[1] == half_k and mix_b.shape[1] == half_k
    o = _mm([(mix_a, half_k, 0), (mix_b, half_k, 0)], [(w_out, half_k, 0), (w_out, half_k, 1)],
            n_tok, d, F32, name="out_proj_even")
    xs, h = _resnorm(xs, n_tok, mod_row, y=o, gate=mod(0, 2), gy=g[1], gn=g[2], sc=mod(0, 3), sh=mod(0, 4))
    y = _ffn(h, ffn_w_gate[0].astype(BF16), ffn_w_up[0].astype(BF16), ffn_w_down[0].astype(BF16))
    g1 = norm_g[1]
    xs, h = _resnorm(xs, n_tok, mod_row, y=y, gate=mod(0, 5), gy=g[3], gn=g1[0], sc=mod(1, 0), sh=mod(1, 1))

    g = g1
    qr, kvr, rr = MLA_Q_RANK, MLA_KV_RANK, MLA_ROPE
    lw = lru_w_a.shape[2] * lru_w_a.shape[3]
    perm = _swap_perm(rr)
    wi = w_in_odd[0]
    o_q, o_kv, o_kr, o_lx, o_ly = 0, qr, qr + kvr, qr + kvr + rr, qr + kvr + rr + lw
    kr_w = wi[:, o_kr:o_kr + rr]
    w_in = jnp.concatenate([wi[:, o_q:o_q + qr], wi[:, o_kv:o_kv + kvr], wi[:, o_lx:o_lx + lw],
                            wi[:, o_ly:o_ly + lw], kr_w, kr_w[:, perm]], axis=-1).astype(BF16)
    assert qr % LANE == 0 and kvr % LANE == 0 and lw % LANE == 0 and 2 * rr == LANE
    p1 = _mm([(h, d, 0)], [(w_in, d, 0)], n_tok, w_in.shape[1], BF16, tn=896, name="in_proj_odd")
    nh = MLA_HEADS
    wq = mla_w_uq[0].reshape(qr, nh, MLA_NOPE + rr)
    wq_pe = wq[:, :, MLA_NOPE:]
    wq = jnp.concatenate([wq[:, :, :MLA_NOPE].reshape(qr, nh * MLA_NOPE),
                          jnp.concatenate([wq_pe, wq_pe[:, :, perm]], axis=-1).reshape(qr, nh * LANE)],
                         axis=-1).astype(BF16)
    wkv = mla_w_ukv[0].reshape(kvr, nh, MLA_NOPE + MLA_V)
    wkv = jnp.concatenate([wkv[:, :, :MLA_NOPE].reshape(kvr, nh * MLA_NOPE),
                           wkv[:, :, MLA_NOPE:].reshape(kvr, nh * MLA_V)], axis=-1).astype(BF16)
    assert qr % kvr == 0
    q = _mm([(p1, qr, 0)], [(wq, qr, 0)], n_lat, wq.shape[1], BF16, norm_g=mla_q_norm[0], name="mla_q")
    kv = _mm([(p1, kvr, qr // kvr)], [(wkv, kvr, 0)], n_tok, wkv.shape[1], BF16, norm_g=mla_kv_norm[0],
             name="mla_kv")
    ar, ac = _rope_angles(s, rr)
    cos4 = jnp.concatenate([jnp.cos(ar), jnp.cos(ar), jnp.cos(ac), jnp.cos(ac)], axis=-1)
    sin4 = jnp.concatenate([-jnp.sin(ar), jnp.sin(ar), -jnp.sin(ac), jnp.sin(ac)], axis=-1)
    tab = jnp.concatenate([cos4, sin4], axis=-1)
    kr_cb = (qr + kvr + 2 * lw) // LANE
    tr = _tile(s, 512, SUBLANE)
    assert n_ctx % tr == 0 or tr % n_ctx == 0
    tr = min(tr, n_ctx) if n_ctx % tr else tr
    nt_lat, nt_seq = n_lat // tr, s // tr
    kpe = pl.pallas_call(
        functools.partial(_kpe_kernel, n_lat=nt_lat),
        out_shape=jax.ShapeDtypeStruct((n_tok, LANE), BF16),
        grid=(n_tok // tr,),
        in_specs=[pl.BlockSpec((tr, LANE), lambda m: (m, kr_cb)),
                  pl.BlockSpec((tr, LANE), lambda m: (jnp.where(m < nt_lat, m % nt_seq, 0), 0))],
        out_specs=pl.BlockSpec((tr, LANE), lambda m: (m, 0)),
        compiler_params=_cparams(("parallel",)),
        name="mla_kpe",
    )(p1, tab)
    att = _mla_attention(q, kv, kpe, tab, b, s, c_len)
    lru = _rglru(p1, lru_conv_w[0], lru_conv_b[0], lru_w_a[0], lru_w_i[0], lru_b_a[0], lru_b_i[0], lru_lambda[0],
                 b, s, c_len, (qr + kvr) // LANE, (qr + kvr + lw) // LANE)
    w_out = w_out_odd[0].astype(BF16)
    half_k = w_out.shape[0] // 2
    assert att.shape[1] == half_k and lru.shape[1] == half_k
    o = _mm([(att, half_k, 0), (lru, half_k, 0)], [(w_out, half_k, 0), (w_out, half_k, 1)],
            n_lat, d, F32, name="out_proj_odd")
    xl, h = _resnorm(xs, n_lat, mod_row, y=o, gate=mod(1, 2), gy=g[1], gn=g[2], sc=mod(1, 3), sh=mod(1, 4))
    comb = _router(h, router_w[0])
    y = _moe(h, comb, moe_w_gate[0].astype(BF16), moe_w_up[0].astype(BF16), moe_w_down[0].astype(BF16))
    (xl,) = _resnorm(xl, n_lat, mod_row, y=y, gate=mod(1, 5), gy=g[3])
    return xl.reshape(b, s, d)
```

```python
import functools

import numpy as np
import jax
import jax.numpy as jnp
from jax import lax
from jax.experimental import pallas as pl
from jax.experimental.pallas import tpu as pltpu

F32 = jnp.float32
BF16 = jnp.bfloat16

EPS = 1e-6
NEG_INF = -1e30
ROPE_BASE = 10000.0
GRID_W = 64
RET_HEADS = 8
RET_CHUNK = 128
NA_HEADS = 16
NA_KH = 8
NA_KW = 16
NA_QROWS = 8
MLA_HEADS = 16
MLA_Q_RANK = 1536
MLA_KV_RANK = 512
MLA_NOPE = 128
MLA_ROPE = 64
MLA_V = 128
LRU_C = 8.0
LRU_CONV = 4
TOP_K = 2

LANE = 128
SUBLANE = 8
VMEM_LIMIT = 52 * 1024 * 1024


def _cparams(sem):
    return pltpu.CompilerParams(dimension_semantics=sem, vmem_limit_bytes=VMEM_LIMIT)


def _tile(n, target, mult=LANE):
    if n <= target:
        return n
    best = None
    for t in range(mult, target + 1, mult):
        if n % t == 0:
            best = t
    assert best is not None, (n, target, mult)
    return best


def _dot(a, b):
    return jnp.dot(a, b, preferred_element_type=F32)


def _dot_nt(a, b):
    return lax.dot_general(a, b, (((1,), (1,)), ((), ())), preferred_element_type=F32)


def _dot_tn(a, b):
    return lax.dot_general(a, b, (((0,), (0,)), ((), ())), preferred_element_type=F32)


def _rms(x, g):
    return x * lax.rsqrt(jnp.mean(x * x, axis=-1, keepdims=True) + EPS) * g


def _mod_kernel(s_ref, w_ref, b_ref, o_ref):
    s = s_ref[...]
    s = (s * jax.nn.sigmoid(s)).astype(BF16)
    o_ref[0] = _dot(s, w_ref[0].astype(BF16)) + b_ref[0]


def _modulation(s_in, w_mod, b_mod):
    nl, d, n = w_mod.shape
    rows = s_in.shape[0]
    tn = _tile(n, 512)
    return pl.pallas_call(
        _mod_kernel,
        out_shape=jax.ShapeDtypeStruct((nl, rows, n), F32),
        grid=(nl, n // tn),
        in_specs=[pl.BlockSpec((rows, d), lambda l, j: (0, 0)),
                  pl.BlockSpec((1, d, tn), lambda l, j: (l, 0, j)),
                  pl.BlockSpec((1, 1, tn), lambda l, j: (l, 0, j))],
        out_specs=pl.BlockSpec((1, rows, tn), lambda l, j: (l, 0, j)),
        compiler_params=_cparams(("arbitrary", "arbitrary")),
        name="modulation",
    )(s_in, w_mod, b_mod.reshape(nl, 1, n))


def _pack_halves(v):
    half = v.shape[1] // 2
    lo = lax.bitcast_convert_type(v[:, :half].astype(BF16).astype(F32), jnp.uint32) >> 16
    hi = lax.bitcast_convert_type(v[:, half:].astype(BF16).astype(F32), jnp.uint32) & jnp.uint32(0xFFFF0000)
    return lo | hi


def _unpack_halves(u):
    lo = lax.bitcast_convert_type(u << 16, F32)
    hi = lax.bitcast_convert_type(u & jnp.uint32(0xFFFF0000), F32)
    return jnp.concatenate([lo, hi], axis=-1)


def _route_top2(logits, n_exp):
    lane = lax.broadcasted_iota(jnp.int32, logits.shape, 1)
    big = jnp.int32(2 ** 30)
    logits = jnp.where(lane < n_exp, logits, -jnp.inf)
    m1 = jnp.max(logits, axis=-1, keepdims=True)
    i1 = jnp.min(jnp.where(logits == m1, lane, big), axis=-1, keepdims=True)
    rest = jnp.where(lane == i1, -jnp.inf, logits)
    m2 = jnp.max(rest, axis=-1, keepdims=True)
    i2 = jnp.min(jnp.where(rest == m2, lane, big), axis=-1, keepdims=True)
    e2 = jnp.exp(m2 - m1)
    g1 = 1.0 / (1.0 + e2)
    g2 = e2 / (1.0 + e2)
    out = jnp.where(lane == 0, i1.astype(F32), 0.0) + jnp.where(lane == 1, i2.astype(F32), 0.0)
    return out + jnp.where(lane == 2, g1, 0.0) + jnp.where(lane == 3, g2, 0.0)


def _resnorm_kernel(*refs, has_y, has_next, n_exp):
    it = iter(refs)
    x_ref = next(it)
    if has_y:
        y_ref, gate_ref, gy_ref = next(it), next(it), next(it)
    if has_next:
        gn_ref, sc_ref, sh_ref = next(it), next(it), next(it)
    if n_exp:
        rw_ref = next(it)
    x = x_ref[...]
    if has_y:
        xo_ref = next(it)
        x = x + gate_ref[0] * _rms(y_ref[...], gy_ref[...])
        xo_ref[...] = x
    if has_next:
        h_ref = next(it)
        h = _rms(x, gn_ref[...]) * (1.0 + sc_ref[0]) + sh_ref[0]
        if n_exp:
            h_ref[...] = _pack_halves(h)
            route_ref = next(it)
            logits = jnp.dot(h, rw_ref[...], preferred_element_type=F32, precision=lax.Precision.HIGHEST)
            route_ref[...] = _route_top2(logits, n_exp)
        else:
            h_ref[...] = h.astype(BF16)


def _resnorm(x, nrows, mod_row, *, y=None, gate=None, gy=None, gn=None, sc=None, sh=None, router_w=None, tr=256):
    d = x.shape[1]
    tr = _tile(nrows, tr, SUBLANE)
    has_y, has_next = y is not None, gn is not None
    n_exp = 0 if router_w is None else router_w.shape[1]
    row = pl.BlockSpec((tr, d), lambda m: (m, 0))
    vec = pl.BlockSpec((1, d), lambda m: (0, 0))
    modv = pl.BlockSpec((1, 1, d), lambda m: (mod_row(m * tr), 0, 0))
    args, specs, outs, ospecs = [x], [row], [], []
    if has_y:
        args += [y, gate, gy.reshape(1, d)]
        specs += [row, modv, vec]
        outs.append(jax.ShapeDtypeStruct((nrows, d), F32))
        ospecs.append(row)
    if has_next:
        args += [gn.reshape(1, d), sc, sh]
        specs += [vec, modv, modv]
        if n_exp:
            assert n_exp <= LANE and TOP_K == 2
            args.append(jnp.pad(router_w.astype(F32), ((0, 0), (0, LANE - n_exp))))
            specs.append(pl.BlockSpec((d, LANE), lambda m: (0, 0)))
            outs += [jax.ShapeDtypeStruct((nrows, d // 2), jnp.uint32), jax.ShapeDtypeStruct((nrows, LANE), F32)]
            ospecs += [pl.BlockSpec((tr, d // 2), lambda m: (m, 0)), pl.BlockSpec((tr, LANE), lambda m: (m, 0))]
        else:
            outs.append(jax.ShapeDtypeStruct((nrows, d), BF16))
            ospecs.append(row)
    res = pl.pallas_call(
        functools.partial(_resnorm_kernel, has_y=has_y, has_next=has_next, n_exp=n_exp),
        out_shape=outs, grid=(nrows // tr,), in_specs=specs, out_specs=ospecs,
        compiler_params=_cparams(("parallel",)),
        name="resnorm",
    )(*args)
    return res


def _mm_kernel(*refs, nparts, has_norm):
    x_refs, w_refs = refs[:nparts], refs[nparts:2 * nparts]
    pos = 2 * nparts
    if has_norm:
        g_ref, o_ref, h_ref = refs[pos], refs[pos + 1], refs[pos + 2]

        @pl.when(pl.program_id(1) == 0)
        def _():
            h_ref[...] = _rms(x_refs[0][...].astype(F32), g_ref[...]).astype(BF16)

        acc = _dot(h_ref[...], w_refs[0][...])
    else:
        o_ref = refs[pos]
        acc = _dot(x_refs[0][...], w_refs[0][...])
        for i in range(1, nparts):
            acc = acc + _dot(x_refs[i][...], w_refs[i][...])
    o_ref[...] = acc.astype(o_ref.dtype)


def _mm(xs, ws, m, n, out_dtype, *, norm_g=None, tm=1024, tn=512, name="mm"):
    tm, tn = _tile(m, tm, SUBLANE), _tile(n, tn)
    has_norm = norm_g is not None
    args, specs = [], []
    for (a, k, cb) in xs:
        args.append(a)
        specs.append(pl.BlockSpec((tm, k), lambda i, j, cb=cb: (i, cb)))
    for (w, k, rb) in ws:
        args.append(w)
        specs.append(pl.BlockSpec((k, tn), lambda i, j, rb=rb: (rb, j)))
    scratch = []
    if has_norm:
        k0 = xs[0][1]
        args.append(norm_g.reshape(1, k0).astype(F32))
        specs.append(pl.BlockSpec((1, k0), lambda i, j: (0, 0)))
        scratch.append(pltpu.VMEM((tm, k0), BF16))
    return pl.pallas_call(
        functools.partial(_mm_kernel, nparts=len(xs), has_norm=has_norm),
        out_shape=jax.ShapeDtypeStruct((m, n), out_dtype),
        grid=(m // tm, n // tn), in_specs=specs,
        out_specs=pl.BlockSpec((tm, tn), lambda i, j: (i, j)),
        scratch_shapes=scratch,
        compiler_params=_cparams(("parallel", "arbitrary")),
        name=name,
    )(*args)


def _swiglu_step(h, wg, wu, wd):
    g = _dot(h, wg)
    u = _dot(h, wu)
    return _dot((g * jax.nn.sigmoid(g) * u).astype(BF16), wd)


def _ffn_kernel(h_ref, wg_ref, wu_ref, wd_ref, o_ref):
    @pl.when(pl.program_id(1) == 0)
    def _():
        o_ref[...] = jnp.zeros_like(o_ref)

    o_ref[...] += _swiglu_step(h_ref[...], wg_ref[...], wu_ref[...], wd_ref[...])


def _ffn(h, wg, wu, wd, *, tm=512, tf=256):
    m, d = h.shape
    f = wg.shape[1]
    tm, tf = _tile(m, tm, SUBLANE), _tile(f, tf)
    return pl.pallas_call(
        _ffn_kernel,
        out_shape=jax.ShapeDtypeStruct((m, d), F32),
        grid=(m // tm, f // tf),
        in_specs=[pl.BlockSpec((tm, d), lambda i, j: (i, 0)),
                  pl.BlockSpec((d, tf), lambda i, j: (0, j)),
                  pl.BlockSpec((d, tf), lambda i, j: (0, j)),
                  pl.BlockSpec((tf, d), lambda i, j: (j, 0))],
        out_specs=pl.BlockSpec((tm, d), lambda i, j: (i, 0)),
        compiler_params=_cparams(("parallel", "arbitrary")),
        name="ffn",
    )(h, wg, wu, wd)


def _moe_kernel(te_ref, nu_ref, src_ref, aid_ref, h_hbm, wg_ref, wu_ref, wd_ref, y_hbm,
                hbuf, hb, acc, stage, gsem, ssem):
    i, f = pl.program_id(0), pl.program_id(1)
    nf = pl.num_programs(1)
    tm = hb.shape[0]
    n_used = nu_ref[0]
    used = i < n_used
    slot = i % 2

    def gather(tile, sl):
        def body(r, _):
            tok = src_ref[tile * tm + r]
            pltpu.make_async_copy(h_hbm.at[pl.ds(tok, 1)], hbuf.at[sl, pl.ds(r, 1)], gsem.at[sl]).start()
            return 0
        lax.fori_loop(0, tm, body, 0)

    def tile_done(buf, sem):
        pltpu.make_async_copy(buf, buf, sem).wait()

    @pl.when(jnp.logical_and(f == 0, used))
    def _():
        @pl.when(i == 0)
        def _():
            gather(0, 0)

        tile_done(hbuf.at[slot], gsem.at[slot])

        @pl.when(i + 1 < n_used)
        def _():
            gather(i + 1, 1 - slot)

        hb[...] = _unpack_halves(hbuf[slot]).astype(BF16)
        acc[...] = jnp.zeros_like(acc)

    @pl.when(used)
    def _():
        acc[...] += _swiglu_step(hb[...], wg_ref[0], wu_ref[0], wd_ref[0])

    @pl.when(f == nf - 1)
    def _():
        @pl.when(i > 0)
        def _():
            tile_done(stage, ssem)

        @pl.when(used)
        def _():
            stage[...] = _pack_halves(acc[...])

        @pl.when(jnp.logical_not(used))
        def _():
            stage[...] = jnp.zeros_like(stage)

        def body(r, _):
            a = aid_ref[i * tm + r]
            pltpu.make_async_copy(stage.at[pl.ds(r, 1)], y_hbm.at[pl.ds(a, 1)], ssem).start()
            return 0
        lax.fori_loop(0, tm, body, 0)

        @pl.when(i == pl.num_programs(0) - 1)
        def _():
            tile_done(stage, ssem)


def _moe_route(route, n_exp, tm):
    n = route.shape[0]
    e = route[:, :TOP_K].astype(jnp.int32).T.reshape(-1)
    onehot = (e[:, None] == jnp.arange(n_exp, dtype=jnp.int32)[None, :]).astype(jnp.int32)
    rank = jnp.sum((jnp.cumsum(onehot, axis=0) - onehot) * onehot, axis=1)
    counts = jnp.sum(onehot, axis=0)
    padded = ((counts + tm - 1) // tm) * tm
    ends = jnp.cumsum(padded)
    starts = ends - padded
    dest = starts[e] + rank
    p_rows = TOP_K * n + n_exp * tm
    n_tiles = p_rows // tm
    tok = jnp.tile(jnp.arange(n, dtype=jnp.int32), TOP_K)
    src = jnp.zeros((p_rows,), jnp.int32).at[dest].set(tok)
    aid = jnp.full((p_rows,), -1, jnp.int32).at[dest].set(jnp.arange(TOP_K * n, dtype=jnp.int32))
    is_pad = aid < 0
    aid = jnp.where(is_pad, TOP_K * n + jnp.cumsum(is_pad.astype(jnp.int32)) - 1, aid)
    n_used = (ends[-1] // tm).astype(jnp.int32)
    tile0 = jnp.arange(n_tiles, dtype=jnp.int32) * tm
    te = jnp.minimum(jnp.sum((tile0[:, None] >= ends[None, :]).astype(jnp.int32), axis=1), n_exp - 1)
    te = jnp.where(jnp.arange(n_tiles) < n_used, te, te[jnp.maximum(n_used - 1, 0)])
    return te, n_used.reshape(1), src, aid


def _moe(hpk, route, wg, wu, wd, *, tm=512, tf=256):
    n, dh = hpk.shape
    d = 2 * dh
    n_exp, _, f = wg.shape
    tm, tf = _tile(n, tm, SUBLANE), _tile(f, tf)
    te, n_used, src, aid = _moe_route(route, n_exp, tm)
    p_rows = src.shape[0]
    nf = f // tf
    frozen = lambda i, j, nu: jnp.where(i < nu[0], j, nf - 1)
    grid_spec = pltpu.PrefetchScalarGridSpec(
        num_scalar_prefetch=4,
        grid=(p_rows // tm, nf),
        in_specs=[pl.BlockSpec(memory_space=pl.ANY),
                  pl.BlockSpec((1, d, tf), lambda i, j, te, nu, s, a: (te[i], 0, frozen(i, j, nu))),
                  pl.BlockSpec((1, d, tf), lambda i, j, te, nu, s, a: (te[i], 0, frozen(i, j, nu))),
                  pl.BlockSpec((1, tf, d), lambda i, j, te, nu, s, a: (te[i], frozen(i, j, nu), 0))],
        out_specs=pl.BlockSpec(memory_space=pl.ANY),
        scratch_shapes=[pltpu.VMEM((2, tm, dh), jnp.uint32), pltpu.VMEM((tm, d), BF16), pltpu.VMEM((tm, d), F32),
                        pltpu.VMEM((tm, dh), jnp.uint32), pltpu.SemaphoreType.DMA((2,)), pltpu.SemaphoreType.DMA(())],
    )
    return pl.pallas_call(
        _moe_kernel,
        out_shape=jax.ShapeDtypeStruct((p_rows, dh), jnp.uint32),
        grid_spec=grid_spec,
        compiler_params=_cparams(("arbitrary", "arbitrary")),
        name="moe",
    )(te, n_used, src, aid, hpk, wg, wu, wd)


def _moe_finish_kernel(x_ref, y1_ref, y2_ref, r_ref, gate_ref, gy_ref, o_ref):
    r = r_ref[...]
    lane = lax.broadcasted_iota(jnp.int32, r.shape, 1)
    g1 = jnp.sum(jnp.where(lane == 2, r, 0.0), axis=-1, keepdims=True)
    g2 = jnp.sum(jnp.where(lane == 3, r, 0.0), axis=-1, keepdims=True)
    y = g1 * _unpack_halves(y1_ref[...]) + g2 * _unpack_halves(y2_ref[...])
    o_ref[...] = x_ref[...] + gate_ref[0] * _rms(y, gy_ref[...])


def _moe_finish(x, ypk, route, mod_row, gate, gy, *, tr=256):
    n, d = x.shape
    tr = _tile(n, tr, SUBLANE)
    nt = n // tr
    return pl.pallas_call(
        _moe_finish_kernel,
        out_shape=jax.ShapeDtypeStruct((n, d), F32),
        grid=(nt,),
        in_specs=[pl.BlockSpec((tr, d), lambda m: (m, 0)),
                  pl.BlockSpec((tr, d // 2), lambda m: (m, 0)),
                  pl.BlockSpec((tr, d // 2), lambda m: (nt + m, 0)),
                  pl.BlockSpec((tr, LANE), lambda m: (m, 0)),
                  pl.BlockSpec((1, 1, d), lambda m: (mod_row(m * tr), 0, 0)),
                  pl.BlockSpec((1, d), lambda m: (0, 0))],
        out_specs=pl.BlockSpec((tr, d), lambda m: (m, 0)),
        compiler_params=_cparams(("parallel",)),
        name="moe_finish",
    )(x, ypk, ypk, route, gate, gy.reshape(1, d))


def _ret_kernel(dec_ref, ql_ref, kl_ref, vl_ref, gl_ref, qc_ref, kc_ref, vc_ref, gc_ref, ca_ref, sa_ref,
                ol_ref, oc_ref, qs, ks, kcs, af_l, ab_l, af_c, ab_c, st_f, st_b, *, chunk, kscale):
    hd = pl.program_id(1)
    s_len, dk = ql_ref.shape
    c_len = qc_ref.shape[0]
    dv = vl_ref.shape[1]
    ch = chunk
    ncl, ncc = s_len // ch, c_len // ch
    half = LANE // 2

    def sl(c):
        return pl.ds(pl.multiple_of(c * ch, ch), ch)

    lane = lax.broadcasted_iota(jnp.int32, (ch, LANE), 1)
    lo = lane < half
    sign = jnp.where(lo, -1.0, 1.0)

    def prep(c, _):
        r = sl(c)
        ca, sa = ca_ref[r, :], sa_ref[r, :]
        car, sar = pltpu.roll(ca, half, 1), pltpu.roll(sa, half, 1)
        cos = (jnp.where(lo, ca, car), jnp.where(lo, car, ca))
        sin = (sign * jnp.where(lo, sa, sar), sign * jnp.where(lo, sar, sa))

        def rope(x):
            parts = []
            for p in range(dk // LANE):
                xp = x[:, p * LANE:(p + 1) * LANE]
                parts.append(xp * cos[p] + pltpu.roll(xp, half, 1) * sin[p])
            return jnp.concatenate(parts, axis=-1)

        qs[r, :] = rope(ql_ref[r, :].astype(F32)).astype(BF16)
        ks[r, :] = (rope(kl_ref[r, :].astype(F32)) * kscale).astype(BF16)
        return 0

    lax.fori_loop(0, ncl, prep, 0)
    kcs[...] = (kc_ref[...].astype(F32) * kscale).astype(BF16)

    def log_gamma(dirn):
        d = jnp.full((1, 1), dec_ref[dirn, hd], F32)
        return -(jnp.maximum(-d, 0.0) + jnp.log1p(jnp.exp(-jnp.abs(d))))

    ii = lax.broadcasted_iota(jnp.int32, (ch, ch), 0).astype(F32)
    jj = lax.broadcasted_iota(jnp.int32, (ch, ch), 1).astype(F32)
    pos_k = lax.broadcasted_iota(jnp.int32, (ch, dk), 0).astype(F32)
    lg_f, lg_b = log_gamma(0), log_gamma(1)
    intra_f = jnp.where(ii >= jj, jnp.exp(lg_f * jnp.maximum(ii - jj, 0.0)), 0.0)
    intra_b = jnp.where(jj >= ii, jnp.exp(lg_b * jnp.maximum(jj - ii, 0.0)), 0.0)
    qdec_f = jnp.exp(lg_f * (pos_k + 1.0))
    kdec_f = jnp.exp(lg_f * (ch - 1.0 - pos_k))
    qdec_b = jnp.exp(lg_b * (ch - pos_k))
    kdec_b = jnp.exp(lg_b * pos_k)
    sdec_f = jnp.exp(lg_f * ch)
    sdec_b = jnp.exp(lg_b * ch)

    def step(q_src, k_src, v_src, acc, st, c, intra, qdec, kdec, sdec):
        r = sl(c)
        q, k, v = q_src[r, :], k_src[r, :], v_src[r, :]
        att = _dot_nt(q, k) * intra
        state = st[...]
        out = _dot(att.astype(BF16), v) + _dot((q.astype(F32) * qdec).astype(BF16), state.astype(BF16))
        acc[r, :] = out
        st[...] = state * sdec + _dot_tn((k.astype(F32) * kdec).astype(BF16), v)

    st_f[...] = jnp.zeros_like(st_f)
    st_b[...] = jnp.zeros_like(st_b)

    def scan(q_src, k_src, v_src, acc_f, acc_b, n):
        def body(c, _):
            step(q_src, k_src, v_src, acc_f, st_f, c, intra_f, qdec_f, kdec_f, sdec_f)
            step(q_src, k_src, v_src, acc_b, st_b, n - 1 - c, intra_b, qdec_b, kdec_b, sdec_b)
            return 0
        lax.fori_loop(0, n, body, 0)

    scan(qc_ref, kcs, vc_ref, af_c, ab_c, ncc)
    scan(qs, ks, vl_ref, af_l, ab_l, ncl)

    def finish(acc_f, acc_b, g_ref, o_ref, n):
        def body(c, _):
            r = sl(c)
            y = acc_f[r, :] + acc_b[r, :]
            y = y * lax.rsqrt(jnp.mean(y * y, axis=-1, keepdims=True) + EPS)
            g = g_ref[r, :].astype(F32)
            o_ref[r, :] = (y * (g * jax.nn.sigmoid(g))).astype(o_ref.dtype)
            return 0
        lax.fori_loop(0, n, body, 0)

    finish(af_l, ab_l, gl_ref, ol_ref, ncl)
    finish(af_c, ab_c, gc_ref, oc_ref, ncc)


def _retention(p0, ret_decay, rope_a, rope_b, b, s, c_len, dk, dv):
    nh = RET_HEADS
    assert dk == 2 * LANE and dv % LANE == 0 and s % RET_CHUNK == 0 and c_len % RET_CHUNK == 0
    assert s % c_len == 0
    cb0 = (b * s) // c_len
    assert dk == dv
    qo, ko, vo, go = 0, nh, 2 * nh, 3 * nh
    lat = lambda off: pl.BlockSpec((s, dk), lambda i, h, off=off: (i, off + h))
    ctx = lambda off: pl.BlockSpec((c_len, dk), lambda i, h, off=off: (cb0 + i, off + h))
    tab = pl.BlockSpec((s, LANE), lambda i, h: (0, 0))
    return pl.pallas_call(
        functools.partial(_ret_kernel, chunk=RET_CHUNK, kscale=dk ** -0.5),
        out_shape=[jax.ShapeDtypeStruct((b * s, nh * dv), BF16),
                   jax.ShapeDtypeStruct((b * c_len, nh * dv), BF16)],
        grid=(b, nh),
        in_specs=[pl.BlockSpec(memory_space=pltpu.SMEM),
                  lat(qo), lat(ko), lat(vo), lat(go), ctx(qo), ctx(ko), ctx(vo), ctx(go), tab, tab],
        out_specs=[pl.BlockSpec((s, dv), lambda i, h: (i, h)),
                   pl.BlockSpec((c_len, dv), lambda i, h: (i, h))],
        scratch_shapes=[pltpu.VMEM((s, dk), BF16), pltpu.VMEM((s, dk), BF16), pltpu.VMEM((c_len, dk), BF16),
                        pltpu.VMEM((s, dv), F32), pltpu.VMEM((s, dv), F32),
                        pltpu.VMEM((c_len, dv), F32), pltpu.VMEM((c_len, dv), F32),
                        pltpu.VMEM((dk, dv), F32), pltpu.VMEM((dk, dv), F32)],
        compiler_params=_cparams(("parallel", "arbitrary")),
        name="retention",
    )(ret_decay.astype(F32), p0, p0, p0, p0, p0, p0, p0, p0, rope_a, rope_b)


def _na_kernel(q_ref, k_ref, v_ref, qc_ref, kc_ref, vc_ref, bias_ref, o_ref, oc_ref, *, qrows, krows, rows, scale):
    blk = pl.program_id(2)
    w = q_ref.shape[0] // qrows
    k0 = jnp.clip(blk * qrows - NA_KH // 2, 0, rows - krows)
    win = pl.ds(pl.multiple_of(k0 * w, w), krows * w)
    q, kc, vc = q_ref[...], kc_ref[...], vc_ref[...]
    s_loc = _dot_nt(q, k_ref[win, :]) * scale + bias_ref[0, 0]
    s_ctx = _dot_nt(q, kc) * scale
    m = jnp.maximum(jnp.max(s_loc, axis=-1, keepdims=True), jnp.max(s_ctx, axis=-1, keepdims=True))
    p_loc, p_ctx = jnp.exp(s_loc - m), jnp.exp(s_ctx - m)
    l = jnp.sum(p_loc, axis=-1, keepdims=True) + jnp.sum(p_ctx, axis=-1, keepdims=True)
    o = _dot(p_loc.astype(BF16), v_ref[win, :]) + _dot(p_ctx.astype(BF16), vc)
    o_ref[...] = (o / l).astype(o_ref.dtype)

    @pl.when(blk == 0)
    def _():
        sc = _dot_nt(qc_ref[...], kc) * scale
        pc = jnp.exp(sc - jnp.max(sc, axis=-1, keepdims=True))
        oc = _dot(pc.astype(BF16), vc) / jnp.sum(pc, axis=-1, keepdims=True)
        oc_ref[...] = oc.astype(oc_ref.dtype)


def _na_bias(rpb, rows):
    w, kh, kw, qr = GRID_W, NA_KH, NA_KW, NA_QROWS
    kr = qr + kh - 1
    qc = np.arange(w)[:, None]
    kc = np.arange(w)[None, :]
    cstart = np.clip(qc - kw // 2, 0, w - kw)
    col_ok = (kc >= cstart) & (kc < cstart + kw)
    dc = np.clip(kc - qc + kw - 1, 0, 2 * kw - 2)
    per_dr = jnp.where(col_ok[None, None], rpb[:, :, dc], NEG_INF)
    r0s = np.array([0, qr, rows - qr])
    k0s = np.clip(r0s - kh // 2, 0, rows - kr)
    qrow = r0s[:, None, None] + np.arange(qr)[None, :, None]
    krow = k0s[:, None, None] + np.arange(kr)[None, None, :]
    win0 = np.clip(qrow - kh // 2, 0, rows - kh)
    row_ok = (krow >= win0) & (krow < win0 + kh)
    dr = np.clip(krow - qrow + kh - 1, 0, 2 * kh - 2)
    full = per_dr[:, dr]
    full = jnp.where(row_ok[None, :, :, :, None, None], full, NEG_INF)
    full = jnp.transpose(full, (0, 1, 2, 4, 3, 5))
    return full.reshape(rpb.shape[0], 3, qr * w, kr * w).astype(F32)


def _neighbourhood(p0, rpb, b, s, c_len, hd, col0):
    nh, w, qr = NA_HEADS, GRID_W, NA_QROWS
    rows = s // w
    kr = qr + NA_KH - 1
    assert rows >= kr and rows % qr == 0 and hd == LANE and s % c_len == 0
    nblk = rows // qr
    cb0 = (b * s) // c_len
    qo, ko, vo = col0, col0 + nh, col0 + 2 * nh
    bias = _na_bias(rpb, rows)
    pat = lambda k: jnp.where(k == 0, 0, jnp.where(k == nblk - 1, 2, 1))
    lat = lambda off: pl.BlockSpec((s, hd), lambda i, h, k, off=off: (i, off + h))
    ctx = lambda off: pl.BlockSpec((c_len, hd), lambda i, h, k, off=off: (cb0 + i, off + h))
    return pl.pallas_call(
        functools.partial(_na_kernel, qrows=qr, krows=kr, rows=rows, scale=hd ** -0.5),
        out_shape=[jax.ShapeDtypeStruct((b * s, nh * hd), BF16),
                   jax.ShapeDtypeStruct((b * c_len, nh * hd), BF16)],
        grid=(b, nh, nblk),
        in_specs=[pl.BlockSpec((qr * w, hd), lambda i, h, k: (i * nblk + k, qo + h)),
                  lat(ko), lat(vo), ctx(qo), ctx(ko), ctx(vo),
                  pl.BlockSpec((1, 1, qr * w, kr * w), lambda i, h, k: (h, pat(k), 0, 0))],
        out_specs=[pl.BlockSpec((qr * w, hd), lambda i, h, k: (i * nblk + k, h)),
                   pl.BlockSpec((c_len, hd), lambda i, h, k: (i, h))],
        compiler_params=_cparams(("parallel", "parallel", "arbitrary")),
        name="neighbourhood",
    )(p0, p0, p0, p0, p0, p0, bias)


def _kpe_kernel(x_ref, t_ref, o_ref, *, n_lat):
    x = x_ref[...].astype(F32)
    half = LANE // 2
    lane = lax.broadcasted_iota(jnp.int32, x.shape, 1)
    r = x * t_ref[...]
    roped = r + pltpu.roll(r, half, 1)
    is_lat = pl.program_id(0) < n_lat
    o_ref[...] = jnp.where(lane < half, jnp.where(is_lat, roped, x), 0.0).astype(o_ref.dtype)


def _mla_kernel(qn_ref, qp_ref, tq_ref, kn_ref, kp_ref, v_ref, knc_ref, kpc_ref, vc_ref, o_ref, k_s, *, scale, tk):
    c_len, s_len = knc_ref.shape[0], kn_ref.shape[0]
    tq = qn_ref.shape[0]
    half = LANE // 2

    @pl.when(pl.program_id(2) == 0)
    def _():
        k_s[0:c_len, 0:LANE] = knc_ref[...]
        k_s[0:c_len, LANE:] = kpc_ref[...]
        k_s[c_len:, 0:LANE] = kn_ref[...]
        k_s[c_len:, LANE:] = kp_ref[...]

    c = scale * np.log2(np.e)
    r = qp_ref[...].astype(F32) * tq_ref[...]
    qpe = r + pltpu.roll(r, half, 1)
    q = jnp.concatenate([(qn_ref[...].astype(F32) * c).astype(BF16), (qpe * c).astype(BF16)], axis=-1)

    def block(k, v, carry):
        m, l, acc = carry
        s = _dot_nt(q, k)
        m_new = jnp.maximum(m, jnp.max(s, axis=-1, keepdims=True))
        alpha = jnp.exp2(m - m_new)
        p = jnp.exp2(s - jnp.tile(m_new, (1, s.shape[1] // LANE)))
        l = alpha * l + jnp.sum(p, axis=-1, keepdims=True)
        acc = alpha * acc + _dot(p.astype(BF16), v)
        return m_new, l, acc

    init = (jnp.full((tq, LANE), NEG_INF, F32), jnp.zeros((tq, LANE), F32), jnp.zeros((tq, LANE), F32))
    carry = block(k_s[0:c_len, :], vc_ref[...], init)

    def body(j, carry):
        kr = pl.ds(pl.multiple_of(c_len + j * tk, LANE), tk)
        vr = pl.ds(pl.multiple_of(j * tk, LANE), tk)
        return block(k_s[kr, :], v_ref[vr, :], carry)

    _, l, acc = lax.fori_loop(0, s_len // tk, body, carry)
    o_ref[...] = (acc / l).astype(o_ref.dtype)


def _mla_attention(q, kv, kpe, tab, b, s, c_len, *, tq=512, tk=512):
    nh = MLA_HEADS
    assert MLA_NOPE == LANE and MLA_V == LANE and 2 * MLA_ROPE == LANE and c_len % LANE == 0
    tq, tk = _tile(s, tq, SUBLANE), _tile(s, tk)
    nq = s // tq
    cb0 = (b * s) // c_len
    return pl.pallas_call(
        functools.partial(_mla_kernel, scale=(MLA_NOPE + MLA_ROPE) ** -0.5, tk=tk),
        out_shape=jax.ShapeDtypeStruct((b * s, nh * MLA_V), BF16),
        grid=(b, nh, nq),
        in_specs=[pl.BlockSpec((tq, LANE), lambda i, h, a: (i * nq + a, h)),
                  pl.BlockSpec((tq, LANE), lambda i, h, a: (i * nq + a, nh + h)),
                  pl.BlockSpec((tq, LANE), lambda i, h, a: (a, 0)),
                  pl.BlockSpec((s, LANE), lambda i, h, a: (i, h)),
                  pl.BlockSpec((s, LANE), lambda i, h, a: (i, 0)),
                  pl.BlockSpec((s, LANE), lambda i, h, a: (i, nh + h)),
                  pl.BlockSpec((c_len, LANE), lambda i, h, a: (cb0 + i, h)),
                  pl.BlockSpec((c_len, LANE), lambda i, h, a: (cb0 + i, 0)),
                  pl.BlockSpec((c_len, LANE), lambda i, h, a: (cb0 + i, nh + h))],
        out_specs=pl.BlockSpec((tq, MLA_V), lambda i, h, a: (i * nq + a, h)),
        scratch_shapes=[pltpu.VMEM((c_len + s, 2 * LANE), BF16)],
        compiler_params=_cparams(("parallel", "parallel", "arbitrary")),
        name="mla_attention",
    )(q, q, tab, kv, kpe, kv, kv, kpe, kv)


def _lru_kernel(xl_ref, xc_ref, yl_ref, cw_ref, cb_ref, wa_ref, wi_ref, ba_ref, bi_ref, lam_ref, o_ref,
                xp_l, xp_c, a_f, u_f, a_b, u_b, *, rows_per_step):
    s_len, c_len = xl_ref.shape[0], xc_ref.shape[0]
    pad = SUBLANE
    left = (LRU_CONV - 1) // 2
    cw, cb = cw_ref[...], cb_ref[...]

    def softplus(x):
        return jnp.maximum(x, 0.0) + jnp.log1p(jnp.exp(-jnp.abs(x)))

    sp = [softplus(-lam_ref[d:d + 1, :]) for d in range(2)]

    def gates(xp, n, f_off, b_off):
        step = min(rows_per_step, n)
        for r0 in range(0, n, step):
            x = cb
            for j in range(LRU_CONV):
                x = x + cw[j:j + 1, :] * xp[pad - left + j + r0: pad - left + j + r0 + step, :]
            xb = x.astype(BF16)
            for d, (a_ref, u_ref, off) in enumerate(((a_f, u_f, f_off), (a_b, u_b, b_off))):
                rg = jax.nn.sigmoid(_dot(xb, wa_ref[d, 0]) + ba_ref[d:d + 1, :])
                ig = jax.nn.sigmoid(_dot(xb, wi_ref[d, 0]) + bi_ref[d:d + 1, :])
                log_a = -LRU_C * rg * sp[d]
                a_ref[off + r0: off + r0 + step, :] = jnp.exp(log_a)
                th = jnp.tanh(log_a)
                u_ref[off + r0: off + r0 + step, :] = jnp.sqrt(-2.0 * th / (1.0 - th)) * ig * x

    zeros = jnp.zeros((pad, xl_ref.shape[1]), F32)
    for xp, src, n in ((xp_l, xl_ref, s_len), (xp_c, xc_ref, c_len)):
        xp[0:pad, :] = zeros
        xp[pad + n: 2 * pad + n, :] = zeros
        xp[pad: pad + n, :] = src[...].astype(F32)
    gates(xp_c, c_len, 0, s_len)
    gates(xp_l, s_len, c_len, 0)

    n_tiles = (s_len + c_len) // SUBLANE
    row = lax.broadcasted_iota(jnp.int32, (SUBLANE, xl_ref.shape[1]), 0)

    def body(i, carry):
        hf, hb = carry
        rf = pl.ds(pl.multiple_of(i * SUBLANE, SUBLANE), SUBLANE)
        a, u = a_f[rf, :], u_f[rf, :]
        for sft in (1, 2, 4):
            m = row >= sft
            u = jnp.where(m, a * pltpu.roll(u, sft, 0) + u, u)
            a = jnp.where(m, a * pltpu.roll(a, sft, 0), a)
        h = a * hf + u
        u_f[rf, :] = h
        hf = h[SUBLANE - 1:SUBLANE, :]
        rb = pl.ds(pl.multiple_of((n_tiles - 1 - i) * SUBLANE, SUBLANE), SUBLANE)
        a, u = a_b[rb, :], u_b[rb, :]
        for sft in (1, 2, 4):
            m = row < SUBLANE - sft
            u = jnp.where(m, a * pltpu.roll(u, SUBLANE - sft, 0) + u, u)
            a = jnp.where(m, a * pltpu.roll(a, SUBLANE - sft, 0), a)
        h = a * hb + u
        u_b[rb, :] = h
        hb = h[0:1, :]
        return hf, hb

    h0 = jnp.zeros((1, xl_ref.shape[1]), F32)
    lax.fori_loop(0, n_tiles, body, (h0, h0), unroll=4)

    step = min(rows_per_step, s_len)
    for r0 in range(0, s_len, step):
        y = yl_ref[r0:r0 + step, :].astype(F32)
        h = u_f[c_len + r0: c_len + r0 + step, :] + u_b[r0:r0 + step, :]
        o_ref[r0:r0 + step, :] = (h * jax.nn.gelu(y)).astype(o_ref.dtype)


def _rglru(p1, conv_w, conv_b, wa, wi, ba, bi, lam, b, s, c_len, x_col0, y_col0):
    nblk, bd = wa.shape[1], wa.shape[2]
    width = nblk * bd
    assert bd == LANE and s % c_len == 0 and (s + c_len) % (4 * SUBLANE) == 0
    cb0 = (b * s) // c_len
    t = s + c_len
    vec2 = pl.BlockSpec((2, bd), lambda i, k: (0, k))
    wspec = pl.BlockSpec((2, 1, bd, bd), lambda i, k: (0, k, 0, 0))
    return pl.pallas_call(
        functools.partial(_lru_kernel, rows_per_step=512),
        out_shape=jax.ShapeDtypeStruct((b * s, width), BF16),
        grid=(b, nblk),
        in_specs=[pl.BlockSpec((s, bd), lambda i, k: (i, x_col0 + k)),
                  pl.BlockSpec((c_len, bd), lambda i, k: (cb0 + i, x_col0 + k)),
                  pl.BlockSpec((s, bd), lambda i, k: (i, y_col0 + k)),
                  pl.BlockSpec((LRU_CONV, bd), lambda i, k: (0, k)),
                  pl.BlockSpec((1, bd), lambda i, k: (0, k)),
                  wspec, wspec, vec2, vec2, vec2],
        out_specs=pl.BlockSpec((s, bd), lambda i, k: (i, k)),
        scratch_shapes=[pltpu.VMEM((s + 2 * SUBLANE, bd), F32), pltpu.VMEM((c_len + 2 * SUBLANE, bd), F32),
                        pltpu.VMEM((t, bd), F32), pltpu.VMEM((t, bd), F32),
                        pltpu.VMEM((t, bd), F32), pltpu.VMEM((t, bd), F32)],
        compiler_params=_cparams(("parallel", "parallel")),
        name="rglru",
    )(p1, p1, p1, conv_w.astype(F32), conv_b.reshape(1, width).astype(F32),
      wa.astype(BF16), wi.astype(BF16), ba.astype(F32), bi.astype(F32), lam.astype(F32))


def _rope_angles(n, rot_dim):
    t = np.arange(n)
    row = (t // GRID_W).astype(np.float32)
    col = (t % GRID_W).astype(np.float32)
    nf = rot_dim // 4
    inv = jnp.asarray(ROPE_BASE, F32) ** (-jnp.arange(nf, dtype=F32) / nf)
    return row[:, None] * inv, col[:, None] * inv


def _swap_perm(rot_dim):
    q = rot_dim // 4
    idx = np.arange(rot_dim).reshape(4, q)
    return idx[[1, 0, 3, 2]].reshape(-1)


def kernel(x, c, ctx, c_ctx, w_mod, b_mod, norm_g, w_in_even, w_out_even, ret_decay, na_rpb, ffn_w_gate, ffn_w_up,
           ffn_w_down, w_in_odd, mla_q_norm, mla_w_uq, mla_kv_norm, mla_w_ukv, lru_conv_w, lru_conv_b, lru_w_a,
           lru_b_a, lru_w_i, lru_b_i, lru_lambda, w_out_odd, router_w, moe_w_gate, moe_w_up, moe_w_down):
    b, s, d = x.shape
    c_len = ctx.shape[1]
    n_lat, n_ctx = b * s, b * c_len
    n_tok = n_lat + n_ctx
    depth = w_mod.shape[0]
    assert depth == 2 and w_in_even.shape[0] == 1 and w_in_odd.shape[0] == 1
    assert n_ctx % SUBLANE == 0 and s % c_len == 0

    mod_rows = -(-(b + 1) // SUBLANE) * SUBLANE
    s_in = jnp.concatenate([c, c_ctx[None], jnp.zeros((mod_rows - b - 1, d), F32)], axis=0)
    mods = _modulation(s_in, w_mod, b_mod).reshape(depth, mod_rows, 6, 1, d)
    mod = lambda l, j: mods[l, :, j]
    mod_row = lambda r0: jnp.where(r0 < n_lat, r0 // s, b)

    xs = jnp.concatenate([x.reshape(n_lat, d), ctx.reshape(n_ctx, d)], axis=0)

    g = norm_g[0]
    (h,) = _resnorm(xs, n_tok, mod_row, gn=g[0], sc=mod(0, 0), sh=mod(0, 1))
    w_in = w_in_even[0].astype(BF16)
    p0 = _mm([(h, d, 0)], [(w_in, d, 0)], n_tok, w_in.shape[1], BF16, name="in_proj_even")
    ret_w = w_out_even.shape[1] - NA_HEADS * LANE
    dk = ret_w // RET_HEADS
    ar, ac = _rope_angles(s, dk)
    rope_a = jnp.concatenate([jnp.cos(ar), jnp.cos(ac)], axis=-1)
    rope_b = jnp.concatenate([jnp.sin(ar), jnp.sin(ac)], axis=-1)
    ret_l, ret_c = _retention(p0, ret_decay[0], rope_a, rope_b, b, s, c_len, dk, dk)
    na_l, na_c = _neighbourhood(p0, na_rpb[0], b, s, c_len, LANE, 4 * ret_w // LANE)
    mix_a = jnp.concatenate([ret_l, ret_c], axis=0)
    mix_b = jnp.concatenate([na_l, na_c], axis=0)
    w_out = w_out_even[0].astype(BF16)
    half_k = w_out.shape[0] // 2
    assert mix_a.shape[1] == half_k and mix_b.shape[1] == half_k
    o = _mm([(mix_a, half_k, 0), (mix_b, half_k, 0)], [(w_out, half_k, 0), (w_out, half_k, 1)],
            n_tok, d, F32, name="out_proj_even")
    xs, h = _resnorm(xs, n_tok, mod_row, y=o, gate=mod(0, 2), gy=g[1], gn=g[2], sc=mod(0, 3), sh=mod(0, 4))
    y = _ffn(h, ffn_w_gate[0].astype(BF16), ffn_w_up[0].astype(BF16), ffn_w_down[0].astype(BF16))
    g1 = norm_g[1]
    xs, h = _resnorm(xs, n_tok, mod_row, y=y, gate=mod(0, 5), gy=g[3], gn=g1[0], sc=mod(1, 0), sh=mod(1, 1))

    g = g1
    qr, kvr, rr = MLA_Q_RANK, MLA_KV_RANK, MLA_ROPE
    lw = lru_w_a.shape[2] * lru_w_a.shape[3]
    perm = _swap_perm(rr)
    wi = w_in_odd[0]
    o_q, o_kv, o_kr, o_lx, o_ly = 0, qr, qr + kvr, qr + kvr + rr, qr + kvr + rr + lw
    kr_w = wi[:, o_kr:o_kr + rr]
    w_in = jnp.concatenate([wi[:, o_q:o_q + qr], wi[:, o_kv:o_kv + kvr], wi[:, o_lx:o_lx + lw],
                            wi[:, o_ly:o_ly + lw], kr_w, kr_w[:, perm]], axis=-1).astype(BF16)
    assert qr % LANE == 0 and kvr % LANE == 0 and lw % LANE == 0 and 2 * rr == LANE
    p1 = _mm([(h, d, 0)], [(w_in, d, 0)], n_tok, w_in.shape[1], BF16, tn=896, name="in_proj_odd")
    nh = MLA_HEADS
    wq = mla_w_uq[0].reshape(qr, nh, MLA_NOPE + rr)
    wq_pe = wq[:, :, MLA_NOPE:]
    wq = jnp.concatenate([wq[:, :, :MLA_NOPE].reshape(qr, nh * MLA_NOPE),
                          jnp.concatenate([wq_pe, wq_pe[:, :, perm]], axis=-1).reshape(qr, nh * LANE)],
                         axis=-1).astype(BF16)
    wkv = mla_w_ukv[0].reshape(kvr, nh, MLA_NOPE + MLA_V)
    wkv = jnp.concatenate([wkv[:, :, :MLA_NOPE].reshape(kvr, nh * MLA_NOPE),
                           wkv[:, :, MLA_NOPE:].reshape(kvr, nh * MLA_V)], axis=-1).astype(BF16)
    assert qr % kvr == 0
    q = _mm([(p1, qr, 0)], [(wq, qr, 0)], n_lat, wq.shape[1], BF16, norm_g=mla_q_norm[0], name="mla_q")
    kv = _mm([(p1, kvr, qr // kvr)], [(wkv, kvr, 0)], n_tok, wkv.shape[1], BF16, norm_g=mla_kv_norm[0],
             name="mla_kv")
    ar, ac = _rope_angles(s, rr)
    cos4 = jnp.concatenate([jnp.cos(ar), jnp.cos(ar), jnp.cos(ac), jnp.cos(ac)], axis=-1)
    sin4 = jnp.concatenate([-jnp.sin(ar), jnp.sin(ar), -jnp.sin(ac), jnp.sin(ac)], axis=-1)
    tab = jnp.concatenate([cos4, sin4], axis=-1)
    kr_cb = (qr + kvr + 2 * lw) // LANE
    tr = _tile(s, 512, SUBLANE)
    assert n_ctx % tr == 0 or tr % n_ctx == 0
    tr = min(tr, n_ctx) if n_ctx % tr else tr
    nt_lat, nt_seq = n_lat // tr, s // tr
    kpe = pl.pallas_call(
        functools.partial(_kpe_kernel, n_lat=nt_lat),
        out_shape=jax.ShapeDtypeStruct((n_tok, LANE), BF16),
        grid=(n_tok // tr,),
        in_specs=[pl.BlockSpec((tr, LANE), lambda m: (m, kr_cb)),
                  pl.BlockSpec((tr, LANE), lambda m: (jnp.where(m < nt_lat, m % nt_seq, 0), 0))],
        out_specs=pl.BlockSpec((tr, LANE), lambda m: (m, 0)),
        compiler_params=_cparams(("parallel",)),
        name="mla_kpe",
    )(p1, tab)
    att = _mla_attention(q, kv, kpe, tab, b, s, c_len)
    lru = _rglru(p1, lru_conv_w[0], lru_conv_b[0], lru_w_a[0], lru_w_i[0], lru_b_a[0], lru_b_i[0], lru_lambda[0],
                 b, s, c_len, (qr + kvr) // LANE, (qr + kvr + lw) // LANE)
    w_out = w_out_odd[0].astype(BF16)
    half_k = w_out.shape[0] // 2
    assert att.shape[1] == half_k and lru.shape[1] == half_k
    o = _mm([(att, half_k, 0), (lru, half_k, 0)], [(w_out, half_k, 0), (w_out, half_k, 1)],
            n_lat, d, F32, name="out_proj_odd")
    xl, hpk, route = _resnorm(xs, n_lat, mod_row, y=o, gate=mod(1, 2), gy=g[1], gn=g[2], sc=mod(1, 3),
                              sh=mod(1, 4), router_w=router_w[0])
    ypk = _moe(hpk, route, moe_w_gate[0].astype(BF16), moe_w_up[0].astype(BF16), moe_w_down[0].astype(BF16))
    xl = _moe_finish(xl, ypk, route, mod_row, mod(1, 5), g[3])
    return xl.reshape(b, s, d)
```

```python
import functools

import numpy as np
import jax
import jax.numpy as jnp
from jax import lax
from jax.experimental import pallas as pl
from jax.experimental.pallas import tpu as pltpu

F32 = jnp.float32
BF16 = jnp.bfloat16

EPS = 1e-6
NEG_INF = -1e30
ROPE_BASE = 10000.0
GRID_W = 64
RET_HEADS = 8
RET_CHUNK = 128
NA_HEADS = 16
NA_KH = 8
NA_KW = 16
NA_QROWS = 8
MLA_HEADS = 16
MLA_Q_RANK = 1536
MLA_KV_RANK = 512
MLA_NOPE = 128
MLA_ROPE = 64
MLA_V = 128
LRU_C = 8.0
LRU_CONV = 4
TOP_K = 2

LANE = 128
SUBLANE = 8
VMEM_LIMIT = 52 * 1024 * 1024


def _cparams(sem):
    return pltpu.CompilerParams(dimension_semantics=sem, vmem_limit_bytes=VMEM_LIMIT)


def _tile(n, target, mult=LANE):
    if n <= target:
        return n
    best = None
    for t in range(mult, target + 1, mult):
        if n % t == 0:
            best = t
    assert best is not None, (n, target, mult)
    return best


def _dot(a, b):
    return jnp.dot(a, b, preferred_element_type=F32)


def _dot_nt(a, b):
    return lax.dot_general(a, b, (((1,), (1,)), ((), ())), preferred_element_type=F32)


def _dot_tn(a, b):
    return lax.dot_general(a, b, (((0,), (0,)), ((), ())), preferred_element_type=F32)


def _sigmoid(x):
    return 0.5 * jnp.tanh(0.5 * x) + 0.5


def _rms(x, g):
    return x * lax.rsqrt(jnp.mean(x * x, axis=-1, keepdims=True) + EPS) * g


def _mod_kernel(s_ref, w_ref, b_ref, o_ref):
    s = s_ref[...]
    s = (s * jax.nn.sigmoid(s)).astype(BF16)
    o_ref[0] = _dot(s, w_ref[0].astype(BF16)) + b_ref[0]


def _modulation(s_in, w_mod, b_mod):
    nl, d, n = w_mod.shape
    rows = s_in.shape[0]
    tn = _tile(n, 512)
    return pl.pallas_call(
        _mod_kernel,
        out_shape=jax.ShapeDtypeStruct((nl, rows, n), F32),
        grid=(nl, n // tn),
        in_specs=[pl.BlockSpec((rows, d), lambda l, j: (0, 0)),
                  pl.BlockSpec((1, d, tn), lambda l, j: (l, 0, j)),
                  pl.BlockSpec((1, 1, tn), lambda l, j: (l, 0, j))],
        out_specs=pl.BlockSpec((1, rows, tn), lambda l, j: (l, 0, j)),
        compiler_params=_cparams(("arbitrary", "arbitrary")),
        name="modulation",
    )(s_in, w_mod, b_mod.reshape(nl, 1, n))


def _pack_halves(v):
    half = v.shape[1] // 2
    lo = lax.bitcast_convert_type(v[:, :half].astype(BF16).astype(F32), jnp.uint32) >> 16
    hi = lax.bitcast_convert_type(v[:, half:].astype(BF16).astype(F32), jnp.uint32) & jnp.uint32(0xFFFF0000)
    return lo | hi


def _unpack_halves(u):
    lo = lax.bitcast_convert_type(u << 16, F32)
    hi = lax.bitcast_convert_type(u & jnp.uint32(0xFFFF0000), F32)
    return jnp.concatenate([lo, hi], axis=-1)


def _route_top2(logits, n_exp):
    lane = lax.broadcasted_iota(jnp.int32, logits.shape, 1)
    big = jnp.int32(2 ** 30)
    logits = jnp.where(lane < n_exp, logits, -jnp.inf)
    m1 = jnp.max(logits, axis=-1, keepdims=True)
    i1 = jnp.min(jnp.where(logits == m1, lane, big), axis=-1, keepdims=True)
    rest = jnp.where(lane == i1, -jnp.inf, logits)
    m2 = jnp.max(rest, axis=-1, keepdims=True)
    i2 = jnp.min(jnp.where(rest == m2, lane, big), axis=-1, keepdims=True)
    e2 = jnp.exp(m2 - m1)
    g1 = 1.0 / (1.0 + e2)
    g2 = e2 / (1.0 + e2)
    out = jnp.where(lane == 0, i1.astype(F32), 0.0) + jnp.where(lane == 1, i2.astype(F32), 0.0)
    return out + jnp.where(lane == 2, g1, 0.0) + jnp.where(lane == 3, g2, 0.0)


def _resnorm_kernel(*refs, has_y, has_next, n_exp):
    it = iter(refs)
    x_ref = next(it)
    if has_y:
        y_ref, gate_ref, gy_ref = next(it), next(it), next(it)
    if has_next:
        gn_ref, sc_ref, sh_ref = next(it), next(it), next(it)
    if n_exp:
        rw_ref = next(it)
    x = x_ref[...]
    if has_y:
        xo_ref = next(it)
        x = x + gate_ref[0] * _rms(y_ref[...].astype(F32), gy_ref[...])
        xo_ref[...] = x
    if has_next:
        h_ref = next(it)
        h = _rms(x, gn_ref[...]) * (1.0 + sc_ref[0]) + sh_ref[0]
        if n_exp:
            h_ref[...] = _pack_halves(h)
            route_ref = next(it)
            logits = jnp.dot(h, rw_ref[...], preferred_element_type=F32, precision=lax.Precision.HIGHEST)
            route_ref[...] = _route_top2(logits, n_exp)
        else:
            h_ref[...] = h.astype(BF16)


def _resnorm(x, nrows, mod_row, *, y=None, gate=None, gy=None, gn=None, sc=None, sh=None, router_w=None, tr=256):
    d = x.shape[1]
    tr = _tile(nrows, tr, SUBLANE)
    has_y, has_next = y is not None, gn is not None
    n_exp = 0 if router_w is None else router_w.shape[1]
    row = pl.BlockSpec((tr, d), lambda m: (m, 0))
    vec = pl.BlockSpec((1, d), lambda m: (0, 0))
    modv = pl.BlockSpec((1, 1, d), lambda m: (mod_row(m * tr), 0, 0))
    args, specs, outs, ospecs = [x], [row], [], []
    if has_y:
        args += [y, gate, gy.reshape(1, d)]
        specs += [row, modv, vec]
        outs.append(jax.ShapeDtypeStruct((nrows, d), F32))
        ospecs.append(row)
    if has_next:
        args += [gn.reshape(1, d), sc, sh]
        specs += [vec, modv, modv]
        if n_exp:
            assert n_exp <= LANE and TOP_K == 2
            args.append(jnp.pad(router_w.astype(F32), ((0, 0), (0, LANE - n_exp))))
            specs.append(pl.BlockSpec((d, LANE), lambda m: (0, 0)))
            outs += [jax.ShapeDtypeStruct((nrows, d // 2), jnp.uint32), jax.ShapeDtypeStruct((nrows, LANE), F32)]
            ospecs += [pl.BlockSpec((tr, d // 2), lambda m: (m, 0)), pl.BlockSpec((tr, LANE), lambda m: (m, 0))]
        else:
            outs.append(jax.ShapeDtypeStruct((nrows, d), BF16))
            ospecs.append(row)
    res = pl.pallas_call(
        functools.partial(_resnorm_kernel, has_y=has_y, has_next=has_next, n_exp=n_exp),
        out_shape=outs, grid=(nrows // tr,), in_specs=specs, out_specs=ospecs,
        compiler_params=_cparams(("parallel",)),
        name="resnorm",
    )(*args)
    return res


def _mm_kernel(*refs, nparts, has_norm):
    x_refs, w_refs = refs[:nparts], refs[nparts:2 * nparts]
    pos = 2 * nparts
    if has_norm:
        g_ref, o_ref, h_ref = refs[pos], refs[pos + 1], refs[pos + 2]

        @pl.when(pl.program_id(1) == 0)
        def _():
            h_ref[...] = _rms(x_refs[0][...].astype(F32), g_ref[...]).astype(BF16)

        acc = _dot(h_ref[...], w_refs[0][...])
    else:
        o_ref = refs[pos]
        acc = _dot(x_refs[0][...], w_refs[0][...])
        for i in range(1, nparts):
            acc = acc + _dot(x_refs[i][...], w_refs[i][...])
    o_ref[...] = acc.astype(o_ref.dtype)


def _mm(xs, ws, m, n, out_dtype, *, norm_g=None, tm=1024, tn=512, name="mm"):
    tm, tn = _tile(m, tm, SUBLANE), _tile(n, tn)
    has_norm = norm_g is not None
    args, specs = [], []
    for (a, k, cb) in xs:
        args.append(a)
        specs.append(pl.BlockSpec((tm, k), lambda i, j, cb=cb: (i, cb)))
    for (w, k, rb) in ws:
        args.append(w)
        specs.append(pl.BlockSpec((k, tn), lambda i, j, rb=rb: (rb, j)))
    scratch = []
    if has_norm:
        k0 = xs[0][1]
        args.append(norm_g.reshape(1, k0).astype(F32))
        specs.append(pl.BlockSpec((1, k0), lambda i, j: (0, 0)))
        scratch.append(pltpu.VMEM((tm, k0), BF16))
    return pl.pallas_call(
        functools.partial(_mm_kernel, nparts=len(xs), has_norm=has_norm),
        out_shape=jax.ShapeDtypeStruct((m, n), out_dtype),
        grid=(m // tm, n // tn), in_specs=specs,
        out_specs=pl.BlockSpec((tm, tn), lambda i, j: (i, j)),
        scratch_shapes=scratch,
        compiler_params=_cparams(("parallel", "arbitrary")),
        name=name,
    )(*args)


def _swiglu_step(h, wg, wu, wd):
    g = _dot(h, wg)
    u = _dot(h, wu)
    return _dot((g * jax.nn.sigmoid(g) * u).astype(BF16), wd)


def _ffn_kernel(h_ref, wg_ref, wu_ref, wd_ref, o_ref):
    @pl.when(pl.program_id(1) == 0)
    def _():
        o_ref[...] = jnp.zeros_like(o_ref)

    o_ref[...] += _swiglu_step(h_ref[...], wg_ref[...], wu_ref[...], wd_ref[...])


def _ffn(h, wg, wu, wd, *, tm=512, tf=256):
    m, d = h.shape
    f = wg.shape[1]
    tm, tf = _tile(m, tm, SUBLANE), _tile(f, tf)
    return pl.pallas_call(
        _ffn_kernel,
        out_shape=jax.ShapeDtypeStruct((m, d), F32),
        grid=(m // tm, f // tf),
        in_specs=[pl.BlockSpec((tm, d), lambda i, j: (i, 0)),
                  pl.BlockSpec((d, tf), lambda i, j: (0, j)),
                  pl.BlockSpec((d, tf), lambda i, j: (0, j)),
                  pl.BlockSpec((tf, d), lambda i, j: (j, 0))],
        out_specs=pl.BlockSpec((tm, d), lambda i, j: (i, 0)),
        compiler_params=_cparams(("parallel", "arbitrary")),
        name="ffn",
    )(h, wg, wu, wd)


def _moe_kernel(te_ref, nu_ref, src_ref, aid_ref, h_hbm, wg_ref, wu_ref, wd_ref, y_hbm,
                hbuf, hb, acc, stage, gsem, ssem):
    i, f = pl.program_id(0), pl.program_id(1)
    nf = pl.num_programs(1)
    tm = hb.shape[0]
    n_used = nu_ref[0]
    used = i < n_used
    slot = i % 2

    def gather(tile, sl):
        def body(r, _):
            tok = src_ref[tile * tm + r]
            pltpu.make_async_copy(h_hbm.at[pl.ds(tok, 1)], hbuf.at[sl, pl.ds(r, 1)], gsem.at[sl]).start()
            return 0
        lax.fori_loop(0, tm, body, 0, unroll=8)

    def tile_done(buf, sem):
        pltpu.make_async_copy(buf, buf, sem).wait()

    @pl.when(jnp.logical_and(f == 0, used))
    def _():
        @pl.when(i == 0)
        def _():
            gather(0, 0)

        tile_done(hbuf.at[slot], gsem.at[slot])

        @pl.when(i + 1 < n_used)
        def _():
            gather(i + 1, 1 - slot)

        hb[...] = _unpack_halves(hbuf[slot]).astype(BF16)
        acc[...] = jnp.zeros_like(acc)

    @pl.when(used)
    def _():
        acc[...] += _swiglu_step(hb[...], wg_ref[0], wu_ref[0], wd_ref[0])

    @pl.when(f == nf - 1)
    def _():
        @pl.when(i > 0)
        def _():
            tile_done(stage, ssem)

        @pl.when(used)
        def _():
            stage[...] = _pack_halves(acc[...])

        @pl.when(jnp.logical_not(used))
        def _():
            stage[...] = jnp.zeros_like(stage)

        def body(r, _):
            a = aid_ref[i * tm + r]
            pltpu.make_async_copy(stage.at[pl.ds(r, 1)], y_hbm.at[pl.ds(a, 1)], ssem).start()
            return 0
        lax.fori_loop(0, tm, body, 0, unroll=8)

        @pl.when(i == pl.num_programs(0) - 1)
        def _():
            tile_done(stage, ssem)


def _moe_route(route, n_exp, tm):
    n = route.shape[0]
    e = route[:, :TOP_K].astype(jnp.int32).T.reshape(-1)
    onehot = (e[:, None] == jnp.arange(n_exp, dtype=jnp.int32)[None, :]).astype(jnp.int32)
    rank = jnp.sum((jnp.cumsum(onehot, axis=0) - onehot) * onehot, axis=1)
    counts = jnp.sum(onehot, axis=0)
    padded = ((counts + tm - 1) // tm) * tm
    ends = jnp.cumsum(padded)
    starts = ends - padded
    dest = starts[e] + rank
    p_rows = TOP_K * n + n_exp * tm
    n_tiles = p_rows // tm
    aid = jnp.full((p_rows,), -1, jnp.int32).at[dest].set(jnp.arange(TOP_K * n, dtype=jnp.int32))
    is_pad = aid < 0
    src = jnp.where(is_pad, 0, aid % n)
    aid = jnp.where(is_pad, TOP_K * n + jnp.cumsum(is_pad.astype(jnp.int32)) - 1, aid)
    n_used = (ends[-1] // tm).astype(jnp.int32)
    tile0 = jnp.arange(n_tiles, dtype=jnp.int32) * tm
    te = jnp.minimum(jnp.sum((tile0[:, None] >= ends[None, :]).astype(jnp.int32), axis=1), n_exp - 1)
    te = jnp.where(jnp.arange(n_tiles) < n_used, te, te[jnp.maximum(n_used - 1, 0)])
    return te, n_used.reshape(1), src, aid


def _moe(hpk, route, wg, wu, wd, *, tm=512, tf=256):
    n, dh = hpk.shape
    d = 2 * dh
    n_exp, _, f = wg.shape
    tm, tf = _tile(n, tm, SUBLANE), _tile(f, tf)
    te, n_used, src, aid = _moe_route(route, n_exp, tm)
    p_rows = src.shape[0]
    nf = f // tf
    frozen = lambda i, j, nu: jnp.where(i < nu[0], j, nf - 1)
    grid_spec = pltpu.PrefetchScalarGridSpec(
        num_scalar_prefetch=4,
        grid=(p_rows // tm, nf),
        in_specs=[pl.BlockSpec(memory_space=pl.ANY),
                  pl.BlockSpec((1, d, tf), lambda i, j, te, nu, s, a: (te[i], 0, frozen(i, j, nu))),
                  pl.BlockSpec((1, d, tf), lambda i, j, te, nu, s, a: (te[i], 0, frozen(i, j, nu))),
                  pl.BlockSpec((1, tf, d), lambda i, j, te, nu, s, a: (te[i], frozen(i, j, nu), 0))],
        out_specs=pl.BlockSpec(memory_space=pl.ANY),
        scratch_shapes=[pltpu.VMEM((2, tm, dh), jnp.uint32), pltpu.VMEM((tm, d), BF16), pltpu.VMEM((tm, d), F32),
                        pltpu.VMEM((tm, dh), jnp.uint32), pltpu.SemaphoreType.DMA((2,)), pltpu.SemaphoreType.DMA(())],
    )
    return pl.pallas_call(
        _moe_kernel,
        out_shape=jax.ShapeDtypeStruct((p_rows, dh), jnp.uint32),
        grid_spec=grid_spec,
        compiler_params=_cparams(("arbitrary", "arbitrary")),
        name="moe",
    )(te, n_used, src, aid, hpk, wg, wu, wd)


def _moe_finish_kernel(x_ref, y1_ref, y2_ref, r_ref, gate_ref, gy_ref, o_ref):
    r = r_ref[...]
    lane = lax.broadcasted_iota(jnp.int32, r.shape, 1)
    g1 = jnp.sum(jnp.where(lane == 2, r, 0.0), axis=-1, keepdims=True)
    g2 = jnp.sum(jnp.where(lane == 3, r, 0.0), axis=-1, keepdims=True)
    y = g1 * _unpack_halves(y1_ref[...]) + g2 * _unpack_halves(y2_ref[...])
    o_ref[...] = x_ref[...] + gate_ref[0] * _rms(y, gy_ref[...])


def _moe_finish(x, ypk, route, mod_row, gate, gy, *, tr=256):
    n, d = x.shape
    tr = _tile(n, tr, SUBLANE)
    nt = n // tr
    return pl.pallas_call(
        _moe_finish_kernel,
        out_shape=jax.ShapeDtypeStruct((n, d), F32),
        grid=(nt,),
        in_specs=[pl.BlockSpec((tr, d), lambda m: (m, 0)),
                  pl.BlockSpec((tr, d // 2), lambda m: (m, 0)),
                  pl.BlockSpec((tr, d // 2), lambda m: (nt + m, 0)),
                  pl.BlockSpec((tr, LANE), lambda m: (m, 0)),
                  pl.BlockSpec((1, 1, d), lambda m: (mod_row(m * tr), 0, 0)),
                  pl.BlockSpec((1, d), lambda m: (0, 0))],
        out_specs=pl.BlockSpec((tr, d), lambda m: (m, 0)),
        compiler_params=_cparams(("parallel",)),
        name="moe_finish",
    )(x, ypk, ypk, route, gate, gy.reshape(1, d))


def _ret_kernel(dec_ref, ql_ref, kl_ref, vl_ref, gl_ref, qc_ref, kc_ref, vc_ref, gc_ref, ca_ref, sa_ref,
                ol_ref, oc_ref, qs, ks, kcs, af_l, ab_l, af_c, ab_c, st_f, st_b, *, chunk, kscale, unroll):
    hd = pl.program_id(1)
    s_len, dk = ql_ref.shape
    c_len = qc_ref.shape[0]
    dv = vl_ref.shape[1]
    ch = chunk
    ncl, ncc = s_len // ch, c_len // ch
    half = LANE // 2

    def sl(c):
        return pl.ds(pl.multiple_of(c * ch, ch), ch)

    lane = lax.broadcasted_iota(jnp.int32, (ch, LANE), 1)
    lo = lane < half
    sign = jnp.where(lo, -1.0, 1.0)

    def prep(c, _):
        r = sl(c)
        ca, sa = ca_ref[r, :], sa_ref[r, :]
        car, sar = pltpu.roll(ca, half, 1), pltpu.roll(sa, half, 1)
        cos = (jnp.where(lo, ca, car), jnp.where(lo, car, ca))
        sin = (sign * jnp.where(lo, sa, sar), sign * jnp.where(lo, sar, sa))

        def rope(x):
            parts = []
            for p in range(dk // LANE):
                xp = x[:, p * LANE:(p + 1) * LANE]
                parts.append(xp * cos[p] + pltpu.roll(xp, half, 1) * sin[p])
            return jnp.concatenate(parts, axis=-1)

        qs[r, :] = rope(ql_ref[r, :].astype(F32)).astype(BF16)
        ks[r, :] = (rope(kl_ref[r, :].astype(F32)) * kscale).astype(BF16)
        return 0

    lax.fori_loop(0, ncl, prep, 0)
    kcs[...] = (kc_ref[...].astype(F32) * kscale).astype(BF16)

    def log_gamma(dirn):
        d = jnp.full((1, 1), dec_ref[dirn, hd], F32)
        return -(jnp.maximum(-d, 0.0) + jnp.log1p(jnp.exp(-jnp.abs(d))))

    ii = lax.broadcasted_iota(jnp.int32, (ch, ch), 0).astype(F32)
    jj = lax.broadcasted_iota(jnp.int32, (ch, ch), 1).astype(F32)
    pos_k = lax.broadcasted_iota(jnp.int32, (ch, dk), 0).astype(F32)
    lg_f, lg_b = log_gamma(0), log_gamma(1)
    intra_f = jnp.where(ii >= jj, jnp.exp(lg_f * jnp.maximum(ii - jj, 0.0)), 0.0)
    intra_b = jnp.where(jj >= ii, jnp.exp(lg_b * jnp.maximum(jj - ii, 0.0)), 0.0)
    qdec_f = jnp.exp(lg_f * (pos_k + 1.0))
    kdec_f = jnp.exp(lg_f * (ch - 1.0 - pos_k))
    qdec_b = jnp.exp(lg_b * (ch - pos_k))
    kdec_b = jnp.exp(lg_b * pos_k)
    sdec_f = jnp.exp(lg_f * ch)
    sdec_b = jnp.exp(lg_b * ch)

    def step(q_src, k_src, v_src, acc, st, c, intra, qdec, kdec, sdec):
        r = sl(c)
        q, k, v = q_src[r, :], k_src[r, :], v_src[r, :]
        att = _dot_nt(q, k) * intra
        state = st[...]
        out = _dot(att.astype(BF16), v) + _dot((q.astype(F32) * qdec).astype(BF16), state.astype(BF16))
        acc[r, :] = out
        st[...] = state * sdec + _dot_tn((k.astype(F32) * kdec).astype(BF16), v)

    st_f[...] = jnp.zeros_like(st_f)
    st_b[...] = jnp.zeros_like(st_b)

    def scan(q_src, k_src, v_src, acc_f, acc_b, n):
        def body(c, _):
            step(q_src, k_src, v_src, acc_f, st_f, c, intra_f, qdec_f, kdec_f, sdec_f)
            step(q_src, k_src, v_src, acc_b, st_b, n - 1 - c, intra_b, qdec_b, kdec_b, sdec_b)
            return 0
        lax.fori_loop(0, n, body, 0, unroll=int(np.gcd(unroll, n)))

    scan(qc_ref, kcs, vc_ref, af_c, ab_c, ncc)
    scan(qs, ks, vl_ref, af_l, ab_l, ncl)

    def finish(acc_f, acc_b, g_ref, o_ref, n):
        def body(c, _):
            r = sl(c)
            y = acc_f[r, :] + acc_b[r, :]
            y = y * lax.rsqrt(jnp.mean(y * y, axis=-1, keepdims=True) + EPS)
            g = g_ref[r, :].astype(F32)
            o_ref[r, :] = (y * (g * jax.nn.sigmoid(g))).astype(o_ref.dtype)
            return 0
        lax.fori_loop(0, n, body, 0)

    finish(af_l, ab_l, gl_ref, ol_ref, ncl)
    finish(af_c, ab_c, gc_ref, oc_ref, ncc)


def _retention(p0, ret_decay, rope_a, rope_b, b, s, c_len, dk, dv, *, unroll=4):
    nh = RET_HEADS
    assert dk == 2 * LANE and dv % LANE == 0 and s % RET_CHUNK == 0 and c_len % RET_CHUNK == 0
    assert s % c_len == 0
    cb0 = (b * s) // c_len
    assert dk == dv
    qo, ko, vo, go = 0, nh, 2 * nh, 3 * nh
    lat = lambda off: pl.BlockSpec((s, dk), lambda i, h, off=off: (i, off + h))
    ctx = lambda off: pl.BlockSpec((c_len, dk), lambda i, h, off=off: (cb0 + i, off + h))
    tab = pl.BlockSpec((s, LANE), lambda i, h: (0, 0))
    return pl.pallas_call(
        functools.partial(_ret_kernel, chunk=RET_CHUNK, kscale=dk ** -0.5, unroll=unroll),
        out_shape=[jax.ShapeDtypeStruct((b * s, nh * dv), BF16),
                   jax.ShapeDtypeStruct((b * c_len, nh * dv), BF16)],
        grid=(b, nh),
        in_specs=[pl.BlockSpec(memory_space=pltpu.SMEM),
                  lat(qo), lat(ko), lat(vo), lat(go), ctx(qo), ctx(ko), ctx(vo), ctx(go), tab, tab],
        out_specs=[pl.BlockSpec((s, dv), lambda i, h: (i, h)),
                   pl.BlockSpec((c_len, dv), lambda i, h: (i, h))],
        scratch_shapes=[pltpu.VMEM((s, dk), BF16), pltpu.VMEM((s, dk), BF16), pltpu.VMEM((c_len, dk), BF16),
                        pltpu.VMEM((s, dv), F32), pltpu.VMEM((s, dv), F32),
                        pltpu.VMEM((c_len, dv), F32), pltpu.VMEM((c_len, dv), F32),
                        pltpu.VMEM((dk, dv), F32), pltpu.VMEM((dk, dv), F32)],
        compiler_params=_cparams(("parallel", "arbitrary")),
        name="retention",
    )(ret_decay.astype(F32), p0, p0, p0, p0, p0, p0, p0, p0, rope_a, rope_b)


def _na_kernel(q_ref, k_ref, v_ref, qc_ref, kc_ref, vc_ref, bias_ref, o_ref, oc_ref, *, qrows, krows, rows, scale,
               nsub):
    nblk = rows // qrows
    step = pl.program_id(2)
    w = q_ref.shape[0] // (qrows * nsub)
    kc, vc = kc_ref[...], vc_ref[...]
    for t in range(nsub):
        blk = step * nsub + t
        k0 = jnp.clip(blk * qrows - NA_KH // 2, 0, rows - krows)
        win = pl.ds(pl.multiple_of(k0 * w, w), krows * w)
        q = q_ref[t * qrows * w:(t + 1) * qrows * w, :]
        pat = jnp.where(blk == 0, 0, jnp.where(blk == nblk - 1, 2, 1))
        s_loc = _dot_nt(q, k_ref[win, :]) * scale + bias_ref[0, pat]
        s_ctx = _dot_nt(q, kc) * scale
        m = jnp.maximum(jnp.max(s_loc, axis=-1, keepdims=True), jnp.max(s_ctx, axis=-1, keepdims=True))
        p_loc, p_ctx = jnp.exp(s_loc - m), jnp.exp(s_ctx - m)
        l = jnp.sum(p_loc, axis=-1, keepdims=True) + jnp.sum(p_ctx, axis=-1, keepdims=True)
        o = _dot(p_loc.astype(BF16), v_ref[win, :]) + _dot(p_ctx.astype(BF16), vc)
        o_ref[t * qrows * w:(t + 1) * qrows * w, :] = (o / l).astype(o_ref.dtype)

    @pl.when(step == 0)
    def _():
        sc = _dot_nt(qc_ref[...], kc) * scale
        pc = jnp.exp(sc - jnp.max(sc, axis=-1, keepdims=True))
        oc = _dot(pc.astype(BF16), vc) / jnp.sum(pc, axis=-1, keepdims=True)
        oc_ref[...] = oc.astype(oc_ref.dtype)


def _na_bias(rpb, rows):
    w, kh, kw, qr = GRID_W, NA_KH, NA_KW, NA_QROWS
    kr = qr + kh - 1
    qc = np.arange(w)[:, None]
    kc = np.arange(w)[None, :]
    cstart = np.clip(qc - kw // 2, 0, w - kw)
    col_ok = (kc >= cstart) & (kc < cstart + kw)
    dc = np.clip(kc - qc + kw - 1, 0, 2 * kw - 2)
    per_dr = jnp.where(col_ok[None, None], rpb[:, :, dc], NEG_INF)
    r0s = np.array([0, qr, rows - qr])
    k0s = np.clip(r0s - kh // 2, 0, rows - kr)
    qrow = r0s[:, None, None] + np.arange(qr)[None, :, None]
    krow = k0s[:, None, None] + np.arange(kr)[None, None, :]
    win0 = np.clip(qrow - kh // 2, 0, rows - kh)
    row_ok = (krow >= win0) & (krow < win0 + kh)
    dr = np.clip(krow - qrow + kh - 1, 0, 2 * kh - 2)
    full = per_dr[:, dr]
    full = jnp.where(row_ok[None, :, :, :, None, None], full, NEG_INF)
    full = jnp.transpose(full, (0, 1, 2, 4, 3, 5))
    return full.reshape(rpb.shape[0], 3, qr * w, kr * w).astype(F32)


def _neighbourhood(p0, rpb, b, s, c_len, hd, col0, *, nsub=8):
    nh, w, qr = NA_HEADS, GRID_W, NA_QROWS
    rows = s // w
    kr = qr + NA_KH - 1
    assert rows >= kr and rows % qr == 0 and hd == LANE and s % c_len == 0
    nblk = rows // qr
    nsub = int(np.gcd(nsub, nblk))
    nstep = nblk // nsub
    cb0 = (b * s) // c_len
    qo, ko, vo = col0, col0 + nh, col0 + 2 * nh
    bias = _na_bias(rpb, rows)
    lat = lambda off: pl.BlockSpec((s, hd), lambda i, h, k, off=off: (i, off + h))
    ctx = lambda off: pl.BlockSpec((c_len, hd), lambda i, h, k, off=off: (cb0 + i, off + h))
    return pl.pallas_call(
        functools.partial(_na_kernel, qrows=qr, krows=kr, rows=rows, scale=hd ** -0.5, nsub=nsub),
        out_shape=[jax.ShapeDtypeStruct((b * s, nh * hd), BF16),
                   jax.ShapeDtypeStruct((b * c_len, nh * hd), BF16)],
        grid=(b, nh, nstep),
        in_specs=[pl.BlockSpec((nsub * qr * w, hd), lambda i, h, k: (i * nstep + k, qo + h)),
                  lat(ko), lat(vo), ctx(qo), ctx(ko), ctx(vo),
                  pl.BlockSpec((1, 3, qr * w, kr * w), lambda i, h, k: (h, 0, 0, 0))],
        out_specs=[pl.BlockSpec((nsub * qr * w, hd), lambda i, h, k: (i * nstep + k, h)),
                   pl.BlockSpec((c_len, hd), lambda i, h, k: (i, h))],
        compiler_params=_cparams(("parallel", "parallel", "arbitrary")),
        name="neighbourhood",
    )(p0, p0, p0, p0, p0, p0, bias)


def _kpe_kernel(x_ref, t_ref, o_ref, *, n_lat):
    x = x_ref[...].astype(F32)
    half = LANE // 2
    lane = lax.broadcasted_iota(jnp.int32, x.shape, 1)
    r = x * t_ref[...]
    roped = r + pltpu.roll(r, half, 1)
    is_lat = pl.program_id(0) < n_lat
    o_ref[...] = jnp.where(lane < half, jnp.where(is_lat, roped, x), 0.0).astype(o_ref.dtype)


def _mla_kernel(qn_ref, qp_ref, tq_ref, kn_ref, kp_ref, v_ref, knc_ref, kpc_ref, vc_ref, o_ref, k_s, *, scale, tk):
    c_len, s_len = knc_ref.shape[0], kn_ref.shape[0]
    tq = qn_ref.shape[0]
    half = LANE // 2

    @pl.when(pl.program_id(2) == 0)
    def _():
        k_s[0:c_len, 0:LANE] = knc_ref[...]
        k_s[0:c_len, LANE:] = kpc_ref[...]
        k_s[c_len:, 0:LANE] = kn_ref[...]
        k_s[c_len:, LANE:] = kp_ref[...]

    c = scale * np.log2(np.e)
    r = qp_ref[...].astype(F32) * tq_ref[...]
    qpe = r + pltpu.roll(r, half, 1)
    q = jnp.concatenate([(qn_ref[...].astype(F32) * c).astype(BF16), (qpe * c).astype(BF16)], axis=-1)

    def update(k, v, carry):
        m, l, acc = carry
        s = _dot_nt(q, k)
        m_new = jnp.maximum(m, jnp.max(s, axis=-1, keepdims=True))
        alpha = jnp.exp2(m - m_new)
        p = jnp.exp2(s - jnp.tile(m_new, (1, s.shape[1] // LANE)))
        l = alpha * l + jnp.sum(p, axis=-1, keepdims=True)
        acc = alpha * acc + _dot(p.astype(BF16), v)
        return m_new, l, acc

    carry = (jnp.full((tq, LANE), NEG_INF, F32), jnp.zeros((tq, LANE), F32), jnp.zeros((tq, LANE), F32))
    carry = update(k_s[0:c_len, :], vc_ref[...], carry)
    for j in range(s_len // tk):
        carry = update(k_s[c_len + j * tk: c_len + (j + 1) * tk, :], v_ref[j * tk:(j + 1) * tk, :], carry)
    _, l, acc = carry
    o_ref[...] = (acc / l).astype(o_ref.dtype)


def _mla_attention(q, kv, kpe, tab, b, s, c_len, *, tq=512, tk=2048):
    nh = MLA_HEADS
    assert MLA_NOPE == LANE and MLA_V == LANE and 2 * MLA_ROPE == LANE and c_len % LANE == 0
    tq, tk = _tile(s, tq, SUBLANE), _tile(s, tk)
    nq = s // tq
    cb0 = (b * s) // c_len
    return pl.pallas_call(
        functools.partial(_mla_kernel, scale=(MLA_NOPE + MLA_ROPE) ** -0.5, tk=tk),
        out_shape=jax.ShapeDtypeStruct((b * s, nh * MLA_V), BF16),
        grid=(b, nh, nq),
        in_specs=[pl.BlockSpec((tq, LANE), lambda i, h, a: (i * nq + a, h)),
                  pl.BlockSpec((tq, LANE), lambda i, h, a: (i * nq + a, nh + h)),
                  pl.BlockSpec((tq, LANE), lambda i, h, a: (a, 0)),
                  pl.BlockSpec((s, LANE), lambda i, h, a: (i, h)),
                  pl.BlockSpec((s, LANE), lambda i, h, a: (i, 0)),
                  pl.BlockSpec((s, LANE), lambda i, h, a: (i, nh + h)),
                  pl.BlockSpec((c_len, LANE), lambda i, h, a: (cb0 + i, h)),
                  pl.BlockSpec((c_len, LANE), lambda i, h, a: (cb0 + i, 0)),
                  pl.BlockSpec((c_len, LANE), lambda i, h, a: (cb0 + i, nh + h))],
        out_specs=pl.BlockSpec((tq, MLA_V), lambda i, h, a: (i * nq + a, h)),
        scratch_shapes=[pltpu.VMEM((c_len + s, 2 * LANE), BF16)],
        compiler_params=_cparams(("parallel", "parallel", "arbitrary")),
        name="mla_attention",
    )(q, q, tab, kv, kpe, kv, kv, kpe, kv)


def _lru_kernel(xl_ref, xc_ref, yl_ref, cw_ref, cb_ref, wa_ref, wi_ref, ba_ref, bi_ref, lam_ref, o_ref,
                xp_l, xp_c, a_f, u_f, a_b, u_b, *, rows_per_step):
    s_len, c_len = xl_ref.shape[0], xc_ref.shape[0]
    pad = SUBLANE
    left = (LRU_CONV - 1) // 2
    cw, cb = cw_ref[...], cb_ref[...]

    def softplus(x):
        return jnp.maximum(x, 0.0) + jnp.log1p(jnp.exp(-jnp.abs(x)))

    sp = [softplus(-lam_ref[d:d + 1, :]) for d in range(2)]

    def gates(xp, n, f_off, b_off):
        step = min(rows_per_step, n)
        for r0 in range(0, n, step):
            x = cb
            for j in range(LRU_CONV):
                x = x + cw[j:j + 1, :] * xp[pad - left + j + r0: pad - left + j + r0 + step, :]
            xb = x.astype(BF16)
            for d, (a_ref, u_ref, off) in enumerate(((a_f, u_f, f_off), (a_b, u_b, b_off))):
                rg = _sigmoid(_dot(xb, wa_ref[d, 0]) + ba_ref[d:d + 1, :])
                ig = _sigmoid(_dot(xb, wi_ref[d, 0]) + bi_ref[d:d + 1, :])
                log_a = -LRU_C * rg * sp[d]
                a_ref[off + r0: off + r0 + step, :] = jnp.exp(log_a)
                th = jnp.tanh(log_a)
                u_ref[off + r0: off + r0 + step, :] = jnp.sqrt(-2.0 * th / (1.0 - th)) * ig * x

    zeros = jnp.zeros((pad, xl_ref.shape[1]), F32)
    for xp, src, n in ((xp_l, xl_ref, s_len), (xp_c, xc_ref, c_len)):
        xp[0:pad, :] = zeros
        xp[pad + n: 2 * pad + n, :] = zeros
        xp[pad: pad + n, :] = src[...].astype(F32)
    gates(xp_c, c_len, 0, s_len)
    gates(xp_l, s_len, c_len, 0)

    n_tiles = (s_len + c_len) // SUBLANE
    row = lax.broadcasted_iota(jnp.int32, (SUBLANE, xl_ref.shape[1]), 0)

    def body(i, carry):
        hf, hb = carry
        rf = pl.ds(pl.multiple_of(i * SUBLANE, SUBLANE), SUBLANE)
        a, u = a_f[rf, :], u_f[rf, :]
        for sft in (1, 2, 4):
            m = row >= sft
            u = jnp.where(m, a * pltpu.roll(u, sft, 0) + u, u)
            a = jnp.where(m, a * pltpu.roll(a, sft, 0), a)
        h = a * hf + u
        u_f[rf, :] = h
        hf = h[SUBLANE - 1:SUBLANE, :]
        rb = pl.ds(pl.multiple_of((n_tiles - 1 - i) * SUBLANE, SUBLANE), SUBLANE)
        a, u = a_b[rb, :], u_b[rb, :]
        for sft in (1, 2, 4):
            m = row < SUBLANE - sft
            u = jnp.where(m, a * pltpu.roll(u, SUBLANE - sft, 0) + u, u)
            a = jnp.where(m, a * pltpu.roll(a, SUBLANE - sft, 0), a)
        h = a * hb + u
        u_b[rb, :] = h
        hb = h[0:1, :]
        return hf, hb

    h0 = jnp.zeros((1, xl_ref.shape[1]), F32)
    lax.fori_loop(0, n_tiles, body, (h0, h0), unroll=4)

    step = min(rows_per_step, s_len)
    for r0 in range(0, s_len, step):
        y = yl_ref[r0:r0 + step, :].astype(F32)
        h = u_f[c_len + r0: c_len + r0 + step, :] + u_b[r0:r0 + step, :]
        o_ref[r0:r0 + step, :] = (h * jax.nn.gelu(y)).astype(o_ref.dtype)


def _rglru(p1, conv_w, conv_b, wa, wi, ba, bi, lam, b, s, c_len, x_col0, y_col0):
    nblk, bd = wa.shape[1], wa.shape[2]
    width = nblk * bd
    assert bd == LANE and s % c_len == 0 and (s + c_len) % (4 * SUBLANE) == 0
    cb0 = (b * s) // c_len
    t = s + c_len
    vec2 = pl.BlockSpec((2, bd), lambda i, k: (0, k))
    wspec = pl.BlockSpec((2, 1, bd, bd), lambda i, k: (0, k, 0, 0))
    return pl.pallas_call(
        functools.partial(_lru_kernel, rows_per_step=512),
        out_shape=jax.ShapeDtypeStruct((b * s, width), BF16),
        grid=(b, nblk),
        in_specs=[pl.BlockSpec((s, bd), lambda i, k: (i, x_col0 + k)),
                  pl.BlockSpec((c_len, bd), lambda i, k: (cb0 + i, x_col0 + k)),
                  pl.BlockSpec((s, bd), lambda i, k: (i, y_col0 + k)),
                  pl.BlockSpec((LRU_CONV, bd), lambda i, k: (0, k)),
                  pl.BlockSpec((1, bd), lambda i, k: (0, k)),
                  wspec, wspec, vec2, vec2, vec2],
        out_specs=pl.BlockSpec((s, bd), lambda i, k: (i, k)),
        scratch_shapes=[pltpu.VMEM((s + 2 * SUBLANE, bd), F32), pltpu.VMEM((c_len + 2 * SUBLANE, bd), F32),
                        pltpu.VMEM((t, bd), F32), pltpu.VMEM((t, bd), F32),
                        pltpu.VMEM((t, bd), F32), pltpu.VMEM((t, bd), F32)],
        compiler_params=_cparams(("parallel", "parallel")),
        name="rglru",
    )(p1, p1, p1, conv_w.astype(F32), conv_b.reshape(1, width).astype(F32),
      wa.astype(BF16), wi.astype(BF16), ba.astype(F32), bi.astype(F32), lam.astype(F32))


def _rope_angles(n, rot_dim):
    t = np.arange(n)
    row = (t // GRID_W).astype(np.float32)
    col = (t % GRID_W).astype(np.float32)
    nf = rot_dim // 4
    inv = jnp.asarray(ROPE_BASE, F32) ** (-jnp.arange(nf, dtype=F32) / nf)
    return row[:, None] * inv, col[:, None] * inv


def _swap_perm(rot_dim):
    q = rot_dim // 4
    idx = np.arange(rot_dim).reshape(4, q)
    return idx[[1, 0, 3, 2]].reshape(-1)


def kernel(x, c, ctx, c_ctx, w_mod, b_mod, norm_g, w_in_even, w_out_even, ret_decay, na_rpb, ffn_w_gate, ffn_w_up,
           ffn_w_down, w_in_odd, mla_q_norm, mla_w_uq, mla_kv_norm, mla_w_ukv, lru_conv_w, lru_conv_b, lru_w_a,
           lru_b_a, lru_w_i, lru_b_i, lru_lambda, w_out_odd, router_w, moe_w_gate, moe_w_up, moe_w_down):
    b, s, d = x.shape
    c_len = ctx.shape[1]
    n_lat, n_ctx = b * s, b * c_len
    n_tok = n_lat + n_ctx
    depth = w_mod.shape[0]
    assert depth == 2 and w_in_even.shape[0] == 1 and w_in_odd.shape[0] == 1
    assert n_ctx % SUBLANE == 0 and s % c_len == 0

    mod_rows = -(-(b + 1) // SUBLANE) * SUBLANE
    s_in = jnp.concatenate([c, c_ctx[None], jnp.zeros((mod_rows - b - 1, d), F32)], axis=0)
    mods = _modulation(s_in, w_mod, b_mod).reshape(depth, mod_rows, 6, 1, d)
    mod = lambda l, j: mods[l, :, j]
    mod_row = lambda r0: jnp.where(r0 < n_lat, r0 // s, b)

    xs = jnp.concatenate([x.reshape(n_lat, d), ctx.reshape(n_ctx, d)], axis=0)

    g = norm_g[0]
    (h,) = _resnorm(xs, n_tok, mod_row, gn=g[0], sc=mod(0, 0), sh=mod(0, 1))
    w_in = w_in_even[0].astype(BF16)
    p0 = _mm([(h, d, 0)], [(w_in, d, 0)], n_tok, w_in.shape[1], BF16, name="in_proj_even")
    ret_w = w_out_even.shape[1] - NA_HEADS * LANE
    dk = ret_w // RET_HEADS
    ar, ac = _rope_angles(s, dk)
    rope_a = jnp.concatenate([jnp.cos(ar), jnp.cos(ac)], axis=-1)
    rope_b = jnp.concatenate([jnp.sin(ar), jnp.sin(ac)], axis=-1)
    ret_l, ret_c = _retention(p0, ret_decay[0], rope_a, rope_b, b, s, c_len, dk, dk)
    na_l, na_c = _neighbourhood(p0, na_rpb[0], b, s, c_len, LANE, 4 * ret_w // LANE)
    mix_a = jnp.concatenate([ret_l, ret_c], axis=0)
    mix_b = jnp.concatenate([na_l, na_c], axis=0)
    w_out = w_out_even[0].astype(BF16)
    half_k = w_out.shape[0] // 2
    assert mix_a.shape[1] == half_k and mix_b.shape[1] == half_k
    o = _mm([(mix_a, half_k, 0), (mix_b, half_k, 0)], [(w_out, half_k, 0), (w_out, half_k, 1)],
            n_tok, d, BF16, name="out_proj_even")
    xs, h = _resnorm(xs, n_tok, mod_row, y=o, gate=mod(0, 2), gy=g[1], gn=g[2], sc=mod(0, 3), sh=mod(0, 4))
    y = _ffn(h, ffn_w_gate[0].astype(BF16), ffn_w_up[0].astype(BF16), ffn_w_down[0].astype(BF16))
    g1 = norm_g[1]
    xs, h = _resnorm(xs, n_tok, mod_row, y=y, gate=mod(0, 5), gy=g[3], gn=g1[0], sc=mod(1, 0), sh=mod(1, 1))

    g = g1
    qr, kvr, rr = MLA_Q_RANK, MLA_KV_RANK, MLA_ROPE
    lw = lru_w_a.shape[2] * lru_w_a.shape[3]
    perm = _swap_perm(rr)
    wi = w_in_odd[0]
    o_q, o_kv, o_kr, o_lx, o_ly = 0, qr, qr + kvr, qr + kvr + rr, qr + kvr + rr + lw
    kr_w = wi[:, o_kr:o_kr + rr]
    w_in = jnp.concatenate([wi[:, o_q:o_q + qr], wi[:, o_kv:o_kv + kvr], wi[:, o_lx:o_lx + lw],
                            wi[:, o_ly:o_ly + lw], kr_w, kr_w[:, perm]], axis=-1).astype(BF16)
    assert qr % LANE == 0 and kvr % LANE == 0 and lw % LANE == 0 and 2 * rr == LANE
    p1 = _mm([(h, d, 0)], [(w_in, d, 0)], n_tok, w_in.shape[1], BF16, tn=896, name="in_proj_odd")
    nh = MLA_HEADS
    wq = mla_w_uq[0].reshape(qr, nh, MLA_NOPE + rr)
    wq_pe = wq[:, :, MLA_NOPE:]
    wq = jnp.concatenate([wq[:, :, :MLA_NOPE].reshape(qr, nh * MLA_NOPE),
                          jnp.concatenate([wq_pe, wq_pe[:, :, perm]], axis=-1).reshape(qr, nh * LANE)],
                         axis=-1).astype(BF16)
    wkv = mla_w_ukv[0].reshape(kvr, nh, MLA_NOPE + MLA_V)
    wkv = jnp.concatenate([wkv[:, :, :MLA_NOPE].reshape(kvr, nh * MLA_NOPE),
                           wkv[:, :, MLA_NOPE:].reshape(kvr, nh * MLA_V)], axis=-1).astype(BF16)
    assert qr % kvr == 0
    q = _mm([(p1, qr, 0)], [(wq, qr, 0)], n_lat, wq.shape[1], BF16, norm_g=mla_q_norm[0], name="mla_q")
    kv = _mm([(p1, kvr, qr // kvr)], [(wkv, kvr, 0)], n_tok, wkv.shape[1], BF16, norm_g=mla_kv_norm[0],
             name="mla_kv")
    ar, ac = _rope_angles(s, rr)
    cos4 = jnp.concatenate([jnp.cos(ar), jnp.cos(ar), jnp.cos(ac), jnp.cos(ac)], axis=-1)
    sin4 = jnp.concatenate([-jnp.sin(ar), jnp.sin(ar), -jnp.sin(ac), jnp.sin(ac)], axis=-1)
    tab = jnp.concatenate([cos4, sin4], axis=-1)
    kr_cb = (qr + kvr + 2 * lw) // LANE
    tr = _tile(s, 512, SUBLANE)
    assert n_ctx % tr == 0 or tr % n_ctx == 0
    tr = min(tr, n_ctx) if n_ctx % tr else tr
    nt_lat, nt_seq = n_lat // tr, s // tr
    kpe = pl.pallas_call(
        functools.partial(_kpe_kernel, n_lat=nt_lat),
        out_shape=jax.ShapeDtypeStruct((n_tok, LANE), BF16),
        grid=(n_tok // tr,),
        in_specs=[pl.BlockSpec((tr, LANE), lambda m: (m, kr_cb)),
                  pl.BlockSpec((tr, LANE), lambda m: (jnp.where(m < nt_lat, m % nt_seq, 0), 0))],
        out_specs=pl.BlockSpec((tr, LANE), lambda m: (m, 0)),
        compiler_params=_cparams(("parallel",)),
        name="mla_kpe",
    )(p1, tab)
    att = _mla_attention(q, kv, kpe, tab, b, s, c_len)
    lru = _rglru(p1, lru_conv_w[0], lru_conv_b[0], lru_w_a[0], lru_w_i[0], lru_b_a[0], lru_b_i[0], lru_lambda[0],
                 b, s, c_len, (qr + kvr) // LANE, (qr + kvr + lw) // LANE)
    w_out = w_out_odd[0].astype(BF16)
    half_k = w_out.shape[0] // 2
    assert att.shape[1] == half_k and lru.shape[1] == half_k
    o = _mm([(att, half_k, 0), (lru, half_k, 0)], [(w_out, half_k, 0), (w_out, half_k, 1)],
            n_lat, d, BF16, name="out_proj_odd")
    xl, hpk, route = _resnorm(xs, n_lat, mod_row, y=o, gate=mod(1, 2), gy=g[1], gn=g[2], sc=mod(1, 3),
                              sh=mod(1, 4), router_w=router_w[0])
    ypk = _moe(hpk, route, moe_w_gate[0].astype(BF16), moe_w_up[0].astype(BF16), moe_w_down[0].astype(BF16))
    xl = _moe_finish(xl, ypk, route, mod_row, mod(1, 5), g[3])
    return xl.reshape(b, s, d)
```

```python
import functools

import numpy as np
import jax
import jax.numpy as jnp
from jax import lax
from jax.experimental import pallas as pl
from jax.experimental.pallas import tpu as pltpu

F32 = jnp.float32
BF16 = jnp.bfloat16

EPS = 1e-6
NEG_INF = -1e30
ROPE_BASE = 10000.0
GRID_W = 64
RET_HEADS = 8
RET_CHUNK = 128
NA_HEADS = 16
NA_KH = 8
NA_KW = 16
NA_QROWS = 8
MLA_HEADS = 16
MLA_Q_RANK = 1536
MLA_KV_RANK = 512
MLA_NOPE = 128
MLA_ROPE = 64
MLA_V = 128
LRU_C = 8.0
LRU_CONV = 4
TOP_K = 2

LANE = 128
SUBLANE = 8
VMEM_LIMIT = 52 * 1024 * 1024


def _cparams(sem):
    return pltpu.CompilerParams(dimension_semantics=sem, vmem_limit_bytes=VMEM_LIMIT)


def _tile(n, target, mult=LANE):
    if n <= target:
        return n
    best = None
    for t in range(mult, target + 1, mult):
        if n % t == 0:
            best = t
    assert best is not None, (n, target, mult)
    return best


def _dot(a, b):
    return jnp.dot(a, b, preferred_element_type=F32)


def _dot_nt(a, b):
    return lax.dot_general(a, b, (((1,), (1,)), ((), ())), preferred_element_type=F32)


def _dot_tn(a, b):
    return lax.dot_general(a, b, (((0,), (0,)), ((), ())), preferred_element_type=F32)


def _sigmoid(x):
    return 0.5 * jnp.tanh(0.5 * x) + 0.5


def _rms(x, g):
    return x * lax.rsqrt(jnp.mean(x * x, axis=-1, keepdims=True) + EPS) * g


def _mod_kernel(s_ref, w_ref, b_ref, o_ref):
    s = s_ref[...]
    s = (s * jax.nn.sigmoid(s)).astype(BF16)
    o_ref[0] = _dot(s, w_ref[0].astype(BF16)) + b_ref[0]


def _modulation(s_in, w_mod, b_mod):
    nl, d, n = w_mod.shape
    rows = s_in.shape[0]
    tn = _tile(n, 512)
    return pl.pallas_call(
        _mod_kernel,
        out_shape=jax.ShapeDtypeStruct((nl, rows, n), F32),
        grid=(nl, n // tn),
        in_specs=[pl.BlockSpec((rows, d), lambda l, j: (0, 0)),
                  pl.BlockSpec((1, d, tn), lambda l, j: (l, 0, j)),
                  pl.BlockSpec((1, 1, tn), lambda l, j: (l, 0, j))],
        out_specs=pl.BlockSpec((1, rows, tn), lambda l, j: (l, 0, j)),
        compiler_params=_cparams(("arbitrary", "arbitrary")),
        name="modulation",
    )(s_in, w_mod, b_mod.reshape(nl, 1, n))


def _pack_halves(v):
    half = v.shape[1] // 2
    lo = lax.bitcast_convert_type(v[:, :half].astype(BF16).astype(F32), jnp.uint32) >> 16
    hi = lax.bitcast_convert_type(v[:, half:].astype(BF16).astype(F32), jnp.uint32) & jnp.uint32(0xFFFF0000)
    return lo | hi


def _unpack_halves(u):
    lo = lax.bitcast_convert_type(u << 16, F32)
    hi = lax.bitcast_convert_type(u & jnp.uint32(0xFFFF0000), F32)
    return jnp.concatenate([lo, hi], axis=-1)


def _route_top2(logits, n_exp):
    lane = lax.broadcasted_iota(jnp.int32, logits.shape, 1)
    big = jnp.int32(2 ** 30)
    logits = jnp.where(lane < n_exp, logits, -jnp.inf)
    m1 = jnp.max(logits, axis=-1, keepdims=True)
    i1 = jnp.min(jnp.where(logits == m1, lane, big), axis=-1, keepdims=True)
    rest = jnp.where(lane == i1, -jnp.inf, logits)
    m2 = jnp.max(rest, axis=-1, keepdims=True)
    i2 = jnp.min(jnp.where(rest == m2, lane, big), axis=-1, keepdims=True)
    e2 = jnp.exp(m2 - m1)
    g1 = 1.0 / (1.0 + e2)
    g2 = e2 / (1.0 + e2)
    out = jnp.where(lane == 0, i1.astype(F32), 0.0) + jnp.where(lane == 1, i2.astype(F32), 0.0)
    return out + jnp.where(lane == 2, g1, 0.0) + jnp.where(lane == 3, g2, 0.0)


def _resnorm_kernel(*refs, has_y, has_next, n_exp, lat_tiles):
    it = iter(refs)
    x_ref = next(it)
    xc_ref = next(it) if lat_tiles else None
    if has_y:
        y_ref, gate_ref, gy_ref = next(it), next(it), next(it)
    if has_next:
        gn_ref, sc_ref, sh_ref = next(it), next(it), next(it)
    if n_exp:
        rw_ref = next(it)
    if lat_tiles:
        x = jnp.where(pl.program_id(0) < lat_tiles, x_ref[0], xc_ref[0])
    else:
        x = x_ref[...]
    if has_y:
        xo_ref = next(it)
        x = x + gate_ref[0] * _rms(y_ref[...].astype(F32), gy_ref[...])
        xo_ref[...] = x
    if has_next:
        h_ref = next(it)
        h = _rms(x, gn_ref[...]) * (1.0 + sc_ref[0]) + sh_ref[0]
        if n_exp:
            h_ref[...] = _pack_halves(h)
            route_ref = next(it)
            logits = jnp.dot(h, rw_ref[...], preferred_element_type=F32, precision=lax.Precision.HIGHEST)
            route_ref[...] = _route_top2(logits, n_exp)
        else:
            h_ref[...] = h.astype(BF16)


def _resnorm(x, nrows, mod_row, *, x_ctx=None, y=None, gate=None, gy=None, gn=None, sc=None, sh=None,
             router_w=None, tr=256):
    d = x.shape[-1]
    tr = _tile(nrows, tr, SUBLANE)
    has_y, has_next = y is not None, gn is not None
    n_exp = 0 if router_w is None else router_w.shape[1]
    row = pl.BlockSpec((tr, d), lambda m: (m, 0))
    vec = pl.BlockSpec((1, d), lambda m: (0, 0))
    modv = pl.BlockSpec((1, 1, d), lambda m: (mod_row(m * tr), 0, 0))
    args, specs, outs, ospecs = [x], [row], [], []
    lat_tiles = 0
    if x_ctx is not None:
        (_, s_len, _), c_len = x.shape, x_ctx.shape[1]
        tr = _tile(int(np.gcd(s_len, c_len)), tr, SUBLANE)
        spt, cpt = s_len // tr, c_len // tr
        lat_tiles = x.shape[0] * spt
        lat = lambda m: jnp.minimum(m, lat_tiles - 1)
        cx = lambda m: jnp.maximum(m - lat_tiles, 0)
        args = [x, x_ctx]
        specs = [pl.BlockSpec((1, tr, d), lambda m: (lat(m) // spt, lat(m) % spt, 0)),
                 pl.BlockSpec((1, tr, d), lambda m: (cx(m) // cpt, cx(m) % cpt, 0))]
        row = pl.BlockSpec((tr, d), lambda m: (m, 0))
        modv = pl.BlockSpec((1, 1, d), lambda m: (mod_row(m * tr), 0, 0))
    if has_y:
        args += [y, gate, gy.reshape(1, d)]
        specs += [row, modv, vec]
        outs.append(jax.ShapeDtypeStruct((nrows, d), F32))
        ospecs.append(row)
    if has_next:
        args += [gn.reshape(1, d), sc, sh]
        specs += [vec, modv, modv]
        if n_exp:
            assert n_exp <= LANE and TOP_K == 2
            args.append(jnp.pad(router_w.astype(F32), ((0, 0), (0, LANE - n_exp))))
            specs.append(pl.BlockSpec((d, LANE), lambda m: (0, 0)))
            outs += [jax.ShapeDtypeStruct((nrows, d // 2), jnp.uint32), jax.ShapeDtypeStruct((nrows, LANE), F32)]
            ospecs += [pl.BlockSpec((tr, d // 2), lambda m: (m, 0)), pl.BlockSpec((tr, LANE), lambda m: (m, 0))]
        else:
            outs.append(jax.ShapeDtypeStruct((nrows, d), BF16))
            ospecs.append(row)
    res = pl.pallas_call(
        functools.partial(_resnorm_kernel, has_y=has_y, has_next=has_next, n_exp=n_exp, lat_tiles=lat_tiles),
        out_shape=outs, grid=(nrows // tr,), in_specs=specs, out_specs=ospecs,
        compiler_params=_cparams(("parallel",)),
        name="resnorm",
    )(*args)
    return res


def _mm_kernel(*refs, nparts, has_norm):
    x_refs, w_refs = refs[:nparts], refs[nparts:2 * nparts]
    pos = 2 * nparts
    if has_norm:
        g_ref, o_ref, h_ref = refs[pos], refs[pos + 1], refs[pos + 2]

        @pl.when(pl.program_id(1) == 0)
        def _():
            h_ref[...] = _rms(x_refs[0][...].astype(F32), g_ref[...]).astype(BF16)

        acc = _dot(h_ref[...], w_refs[0][...])
    else:
        o_ref = refs[pos]
        acc = _dot(x_refs[0][...], w_refs[0][...])
        for i in range(1, nparts):
            acc = acc + _dot(x_refs[i][...], w_refs[i][...])
    o_ref[...] = acc.astype(o_ref.dtype)


def _mm(xs, ws, m, n, out_dtype, *, norm_g=None, tm=1024, tn=512, name="mm"):
    tm, tn = _tile(m, tm, SUBLANE), _tile(n, tn)
    has_norm = norm_g is not None
    args, specs = [], []
    for (a, k, cb) in xs:
        args.append(a)
        specs.append(pl.BlockSpec((tm, k), lambda i, j, cb=cb: (i, cb)))
    for (w, k, rb) in ws:
        args.append(w)
        specs.append(pl.BlockSpec((k, tn), lambda i, j, rb=rb: (rb, j)))
    scratch = []
    if has_norm:
        k0 = xs[0][1]
        args.append(norm_g.reshape(1, k0).astype(F32))
        specs.append(pl.BlockSpec((1, k0), lambda i, j: (0, 0)))
        scratch.append(pltpu.VMEM((tm, k0), BF16))
    return pl.pallas_call(
        functools.partial(_mm_kernel, nparts=len(xs), has_norm=has_norm),
        out_shape=jax.ShapeDtypeStruct((m, n), out_dtype),
        grid=(m // tm, n // tn), in_specs=specs,
        out_specs=pl.BlockSpec((tm, tn), lambda i, j: (i, j)),
        scratch_shapes=scratch,
        compiler_params=_cparams(("parallel", "arbitrary")),
        name=name,
    )(*args)


def _swiglu_step(h, wg, wu, wd):
    g = _dot(h, wg)
    u = _dot(h, wu)
    return _dot((g * jax.nn.sigmoid(g) * u).astype(BF16), wd)


def _ffn_kernel(h_ref, wg_ref, wu_ref, wd_ref, o_ref):
    @pl.when(pl.program_id(1) == 0)
    def _():
        o_ref[...] = jnp.zeros_like(o_ref)

    o_ref[...] += _swiglu_step(h_ref[...], wg_ref[...], wu_ref[...], wd_ref[...])


def _ffn(h, wg, wu, wd, *, tm=512, tf=256):
    m, d = h.shape
    f = wg.shape[1]
    tm, tf = _tile(m, tm, SUBLANE), _tile(f, tf)
    return pl.pallas_call(
        _ffn_kernel,
        out_shape=jax.ShapeDtypeStruct((m, d), F32),
        grid=(m // tm, f // tf),
        in_specs=[pl.BlockSpec((tm, d), lambda i, j: (i, 0)),
                  pl.BlockSpec((d, tf), lambda i, j: (0, j)),
                  pl.BlockSpec((d, tf), lambda i, j: (0, j)),
                  pl.BlockSpec((tf, d), lambda i, j: (j, 0))],
        out_specs=pl.BlockSpec((tm, d), lambda i, j: (i, 0)),
        compiler_params=_cparams(("parallel", "arbitrary")),
        name="ffn",
    )(h, wg, wu, wd)


def _moe_kernel(te_ref, nu_ref, src_ref, aid_ref, h_hbm, wg_ref, wu_ref, wd_ref, y_hbm,
                hbuf, hb, acc, stage, gsem, ssem, *, nf, p_rows):
    i, f = pl.program_id(0), pl.program_id(1)
    nt = pl.num_programs(0)
    tm = hb.shape[0]
    rows = tm // nf
    n_used = nu_ref[0]
    used = i < n_used
    slot = i % 2
    r0 = f * rows

    def gather_row(tile, sl, r):
        tok = src_ref[tile * tm + r]
        pltpu.make_async_copy(h_hbm.at[pl.ds(tok, 1)], hbuf.at[sl, pl.ds(r, 1)], gsem.at[sl]).start()

    def scatter_row(sl, r, dst):
        pltpu.make_async_copy(stage.at[sl, pl.ds(r, 1)], y_hbm.at[pl.ds(dst, 1)], ssem.at[sl]).start()

    def tile_done(buf, sem):
        pltpu.make_async_copy(buf, buf, sem).wait()

    def scatter_slice():
        prev = jnp.maximum(i - 1, 0)
        for r in range(rows):
            dst = jnp.where(i >= 1, aid_ref[prev * tm + r0 + r], p_rows + r0 + r)
            scatter_row(1 - slot, r0 + r, dst)

    @pl.when(jnp.logical_and(f == 0, i == 0))
    def _():
        stage[1] = jnp.zeros(stage.shape[1:], stage.dtype)

        def body(r, _):
            gather_row(0, 0, r)
            return 0
        lax.fori_loop(0, tm, body, 0, unroll=8)

    @pl.when(jnp.logical_and(f == 0, i <= n_used))
    def _():
        tile_done(hbuf.at[slot], gsem.at[slot])

    @pl.when(jnp.logical_and(f == 0, used))
    def _():
        hb[...] = _unpack_halves(hbuf[slot]).astype(BF16)
        acc[...] = jnp.zeros_like(acc)

    @pl.when(used)
    def _():
        for r in range(rows):
            gather_row(i + 1, 1 - slot, r0 + r)
        scatter_slice()
        acc[...] += _swiglu_step(hb[...], wg_ref[0], wu_ref[0], wd_ref[0])

    @pl.when(jnp.logical_not(used))
    def _():
        scatter_slice()

    @pl.when(f == nf - 1)
    def _():
        @pl.when(i >= 1)
        def _():
            tile_done(stage.at[slot], ssem.at[slot])

        @pl.when(used)
        def _():
            stage[slot] = _pack_halves(acc[...])

        @pl.when(jnp.logical_not(used))
        def _():
            stage[slot] = jnp.zeros(stage.shape[1:], stage.dtype)

        @pl.when(i == nt - 1)
        def _():
            def body(r, _):
                scatter_row(slot, r, aid_ref[i * tm + r])
                return 0
            lax.fori_loop(0, tm, body, 0, unroll=8)
            tile_done(stage.at[slot], ssem.at[slot])
            tile_done(stage.at[1 - slot], ssem.at[1 - slot])


def _moe_route(route, n_exp, tm):
    n = route.shape[0]
    e = route[:, :TOP_K].astype(jnp.int32).T.reshape(-1)
    onehot = (e[:, None] == jnp.arange(n_exp, dtype=jnp.int32)[None, :]).astype(jnp.int32)
    rank = jnp.sum((jnp.cumsum(onehot, axis=0) - onehot) * onehot, axis=1)
    counts = jnp.sum(onehot, axis=0)
    padded = ((counts + tm - 1) // tm) * tm
    ends = jnp.cumsum(padded)
    starts = ends - padded
    dest = starts[e] + rank
    p_rows = TOP_K * n + n_exp * tm
    n_tiles = p_rows // tm
    aid = jnp.full((p_rows,), -1, jnp.int32).at[dest].set(jnp.arange(TOP_K * n, dtype=jnp.int32))
    is_pad = aid < 0
    src = jnp.where(is_pad, 0, aid % n)
    aid = jnp.where(is_pad, TOP_K * n + jnp.cumsum(is_pad.astype(jnp.int32)) - 1, aid)
    n_used = (ends[-1] // tm).astype(jnp.int32)
    tile0 = jnp.arange(n_tiles, dtype=jnp.int32) * tm
    te = jnp.minimum(jnp.sum((tile0[:, None] >= ends[None, :]).astype(jnp.int32), axis=1), n_exp - 1)
    te = jnp.where(jnp.arange(n_tiles) < n_used, te, te[jnp.maximum(n_used - 1, 0)])
    return te, n_used.reshape(1), src, aid


def _moe(hpk, route, wg, wu, wd, *, tm=512, tf=256):
    n, dh = hpk.shape
    d = 2 * dh
    n_exp, _, f = wg.shape
    tm, tf = _tile(n, tm, SUBLANE), _tile(f, tf)
    te, n_used, src, aid = _moe_route(route, n_exp, tm)
    p_rows = src.shape[0]
    nf = f // tf
    assert tm % nf == 0
    frozen = lambda i, j, nu: jnp.where(i < nu[0], j, nf - 1)
    grid_spec = pltpu.PrefetchScalarGridSpec(
        num_scalar_prefetch=4,
        grid=(p_rows // tm, nf),
        in_specs=[pl.BlockSpec(memory_space=pl.ANY),
                  pl.BlockSpec((1, d, tf), lambda i, j, te, nu, s, a: (te[i], 0, frozen(i, j, nu))),
                  pl.BlockSpec((1, d, tf), lambda i, j, te, nu, s, a: (te[i], 0, frozen(i, j, nu))),
                  pl.BlockSpec((1, tf, d), lambda i, j, te, nu, s, a: (te[i], frozen(i, j, nu), 0))],
        out_specs=pl.BlockSpec(memory_space=pl.ANY),
        scratch_shapes=[pltpu.VMEM((2, tm, dh), jnp.uint32), pltpu.VMEM((tm, d), BF16), pltpu.VMEM((tm, d), F32),
                        pltpu.VMEM((2, tm, dh), jnp.uint32), pltpu.SemaphoreType.DMA((2,)),
                        pltpu.SemaphoreType.DMA((2,))],
    )
    return pl.pallas_call(
        functools.partial(_moe_kernel, nf=nf, p_rows=p_rows),
        out_shape=jax.ShapeDtypeStruct((p_rows + tm, dh), jnp.uint32),
        grid_spec=grid_spec,
        compiler_params=_cparams(("arbitrary", "arbitrary")),
        name="moe",
    )(te, n_used, src, aid, hpk, wg, wu, wd)


def _moe_finish_kernel(x_ref, y1_ref, y2_ref, r_ref, gate_ref, gy_ref, o_ref):
    r = r_ref[...]
    lane = lax.broadcasted_iota(jnp.int32, r.shape, 1)
    g1 = jnp.sum(jnp.where(lane == 2, r, 0.0), axis=-1, keepdims=True)
    g2 = jnp.sum(jnp.where(lane == 3, r, 0.0), axis=-1, keepdims=True)
    y = g1 * _unpack_halves(y1_ref[...]) + g2 * _unpack_halves(y2_ref[...])
    o_ref[...] = x_ref[...] + gate_ref[0] * _rms(y, gy_ref[...])


def _moe_finish(x, ypk, route, mod_row, gate, gy, *, tr=256):
    n, d = x.shape
    tr = _tile(n, tr, SUBLANE)
    nt = n // tr
    return pl.pallas_call(
        _moe_finish_kernel,
        out_shape=jax.ShapeDtypeStruct((n, d), F32),
        grid=(nt,),
        in_specs=[pl.BlockSpec((tr, d), lambda m: (m, 0)),
                  pl.BlockSpec((tr, d // 2), lambda m: (m, 0)),
                  pl.BlockSpec((tr, d // 2), lambda m: (nt + m, 0)),
                  pl.BlockSpec((tr, LANE), lambda m: (m, 0)),
                  pl.BlockSpec((1, 1, d), lambda m: (mod_row(m * tr), 0, 0)),
                  pl.BlockSpec((1, d), lambda m: (0, 0))],
        out_specs=pl.BlockSpec((tr, d), lambda m: (m, 0)),
        compiler_params=_cparams(("parallel",)),
        name="moe_finish",
    )(x, ypk, ypk, route, gate, gy.reshape(1, d))


def _ret_kernel(dec_ref, ql_ref, kl_ref, vl_ref, gl_ref, qc_ref, kc_ref, vc_ref, gc_ref, ca_ref, sa_ref,
                ol_ref, oc_ref, qs, ks, kcs, af_l, ab_l, af_c, ab_c, st_f, st_b, *, chunk, kscale, unroll):
    hd = pl.program_id(1)
    s_len, dk = ql_ref.shape
    c_len = qc_ref.shape[0]
    dv = vl_ref.shape[1]
    ch = chunk
    ncl, ncc = s_len // ch, c_len // ch
    half = LANE // 2

    def sl(c):
        return pl.ds(pl.multiple_of(c * ch, ch), ch)

    lane = lax.broadcasted_iota(jnp.int32, (ch, LANE), 1)
    lo = lane < half
    sign = jnp.where(lo, -1.0, 1.0)

    def prep(c, _):
        r = sl(c)
        ca, sa = ca_ref[r, :], sa_ref[r, :]
        car, sar = pltpu.roll(ca, half, 1), pltpu.roll(sa, half, 1)
        cos = (jnp.where(lo, ca, car), jnp.where(lo, car, ca))
        sin = (sign * jnp.where(lo, sa, sar), sign * jnp.where(lo, sar, sa))

        def rope(x):
            parts = []
            for p in range(dk // LANE):
                xp = x[:, p * LANE:(p + 1) * LANE]
                parts.append(xp * cos[p] + pltpu.roll(xp, half, 1) * sin[p])
            return jnp.concatenate(parts, axis=-1)

        qs[r, :] = rope(ql_ref[r, :].astype(F32)).astype(BF16)
        ks[r, :] = (rope(kl_ref[r, :].astype(F32)) * kscale).astype(BF16)
        return 0

    lax.fori_loop(0, ncl, prep, 0)
    kcs[...] = (kc_ref[...].astype(F32) * kscale).astype(BF16)

    def log_gamma(dirn):
        d = jnp.full((1, 1), dec_ref[dirn, hd], F32)
        return -(jnp.maximum(-d, 0.0) + jnp.log1p(jnp.exp(-jnp.abs(d))))

    ii = lax.broadcasted_iota(jnp.int32, (ch, ch), 0).astype(F32)
    jj = lax.broadcasted_iota(jnp.int32, (ch, ch), 1).astype(F32)
    pos_k = lax.broadcasted_iota(jnp.int32, (ch, dk), 0).astype(F32)
    lg_f, lg_b = log_gamma(0), log_gamma(1)
    intra_f = jnp.where(ii >= jj, jnp.exp(lg_f * jnp.maximum(ii - jj, 0.0)), 0.0)
    intra_b = jnp.where(jj >= ii, jnp.exp(lg_b * jnp.maximum(jj - ii, 0.0)), 0.0)
    qdec_f = jnp.exp(lg_f * (pos_k + 1.0))
    kdec_f = jnp.exp(lg_f * (ch - 1.0 - pos_k))
    qdec_b = jnp.exp(lg_b * (ch - pos_k))
    kdec_b = jnp.exp(lg_b * pos_k)
    sdec_f = jnp.exp(lg_f * ch)
    sdec_b = jnp.exp(lg_b * ch)

    def step(q_src, k_src, v_src, acc, st, c, intra, qdec, kdec, sdec):
        r = sl(c)
        q, k, v = q_src[r, :], k_src[r, :], v_src[r, :]
        att = _dot_nt(q, k) * intra
        state = st[...]
        out = _dot(att.astype(BF16), v) + _dot((q.astype(F32) * qdec).astype(BF16), state.astype(BF16))
        acc[r, :] = out
        st[...] = state * sdec + _dot_tn((k.astype(F32) * kdec).astype(BF16), v)

    st_f[...] = jnp.zeros_like(st_f)
    st_b[...] = jnp.zeros_like(st_b)

    def scan(q_src, k_src, v_src, acc_f, acc_b, n):
        def body(c, _):
            step(q_src, k_src, v_src, acc_f, st_f, c, intra_f, qdec_f, kdec_f, sdec_f)
            step(q_src, k_src, v_src, acc_b, st_b, n - 1 - c, intra_b, qdec_b, kdec_b, sdec_b)
            return 0
        lax.fori_loop(0, n, body, 0, unroll=int(np.gcd(unroll, n)))

    scan(qc_ref, kcs, vc_ref, af_c, ab_c, ncc)
    scan(qs, ks, vl_ref, af_l, ab_l, ncl)

    def finish(acc_f, acc_b, g_ref, o_ref, n):
        def body(c, _):
            r = sl(c)
            y = acc_f[r, :] + acc_b[r, :]
            y = y * lax.rsqrt(jnp.mean(y * y, axis=-1, keepdims=True) + EPS)
            g = g_ref[r, :].astype(F32)
            o_ref[r, :] = (y * (g * jax.nn.sigmoid(g))).astype(o_ref.dtype)
            return 0
        lax.fori_loop(0, n, body, 0)

    finish(af_l, ab_l, gl_ref, ol_ref, ncl)
    finish(af_c, ab_c, gc_ref, oc_ref, ncc)


def _retention(p0, ret_decay, rope_a, rope_b, b, s, c_len, dk, dv, *, unroll=4):
    nh = RET_HEADS
    assert dk == 2 * LANE and dv % LANE == 0 and s % RET_CHUNK == 0 and c_len % RET_CHUNK == 0
    assert s % c_len == 0
    cb0 = (b * s) // c_len
    assert dk == dv
    qo, ko, vo, go = 0, nh, 2 * nh, 3 * nh
    lat = lambda off: pl.BlockSpec((s, dk), lambda i, h, off=off: (i, off + h))
    ctx = lambda off: pl.BlockSpec((c_len, dk), lambda i, h, off=off: (cb0 + i, off + h))
    tab = pl.BlockSpec((s, LANE), lambda i, h: (0, 0))
    return pl.pallas_call(
        functools.partial(_ret_kernel, chunk=RET_CHUNK, kscale=dk ** -0.5, unroll=unroll),
        out_shape=[jax.ShapeDtypeStruct((b * s, nh * dv), BF16),
                   jax.ShapeDtypeStruct((b * c_len, nh * dv), BF16)],
        grid=(b, nh),
        in_specs=[pl.BlockSpec(memory_space=pltpu.SMEM),
                  lat(qo), lat(ko), lat(vo), lat(go), ctx(qo), ctx(ko), ctx(vo), ctx(go), tab, tab],
        out_specs=[pl.BlockSpec((s, dv), lambda i, h: (i, h)),
                   pl.BlockSpec((c_len, dv), lambda i, h: (i, h))],
        scratch_shapes=[pltpu.VMEM((s, dk), BF16), pltpu.VMEM((s, dk), BF16), pltpu.VMEM((c_len, dk), BF16),
                        pltpu.VMEM((s, dv), F32), pltpu.VMEM((s, dv), F32),
                        pltpu.VMEM((c_len, dv), F32), pltpu.VMEM((c_len, dv), F32),
                        pltpu.VMEM((dk, dv), F32), pltpu.VMEM((dk, dv), F32)],
        compiler_params=_cparams(("parallel", "arbitrary")),
        name="retention",
    )(ret_decay.astype(F32), p0, p0, p0, p0, p0, p0, p0, p0, rope_a, rope_b)


def _na_kernel(q_ref, k_ref, v_ref, qc_ref, kc_ref, vc_ref, bias_ref, o_ref, oc_ref, *, qrows, krows, rows, scale,
               nsub):
    nblk = rows // qrows
    step = pl.program_id(2)
    w = q_ref.shape[0] // (qrows * nsub)
    kc, vc = kc_ref[...], vc_ref[...]
    for t in range(nsub):
        blk = step * nsub + t
        k0 = jnp.clip(blk * qrows - NA_KH // 2, 0, rows - krows)
        win = pl.ds(pl.multiple_of(k0 * w, w), krows * w)
        q = q_ref[t * qrows * w:(t + 1) * qrows * w, :]
        pat = jnp.where(blk == 0, 0, jnp.where(blk == nblk - 1, 2, 1))
        s_loc = _dot_nt(q, k_ref[win, :]) * scale + bias_ref[0, pat]
        s_ctx = _dot_nt(q, kc) * scale
        m = jnp.maximum(jnp.max(s_loc, axis=-1, keepdims=True), jnp.max(s_ctx, axis=-1, keepdims=True))
        p_loc, p_ctx = jnp.exp(s_loc - m), jnp.exp(s_ctx - m)
        l = jnp.sum(p_loc, axis=-1, keepdims=True) + jnp.sum(p_ctx, axis=-1, keepdims=True)
        o = _dot(p_loc.astype(BF16), v_ref[win, :]) + _dot(p_ctx.astype(BF16), vc)
        o_ref[t * qrows * w:(t + 1) * qrows * w, :] = (o / l).astype(o_ref.dtype)

    @pl.when(step == 0)
    def _():
        sc = _dot_nt(qc_ref[...], kc) * scale
        pc = jnp.exp(sc - jnp.max(sc, axis=-1, keepdims=True))
        oc = _dot(pc.astype(BF16), vc) / jnp.sum(pc, axis=-1, keepdims=True)
        oc_ref[...] = oc.astype(oc_ref.dtype)


def _na_bias(rpb, rows):
    w, kh, kw, qr = GRID_W, NA_KH, NA_KW, NA_QROWS
    kr = qr + kh - 1
    qc = np.arange(w)[:, None]
    kc = np.arange(w)[None, :]
    cstart = np.clip(qc - kw // 2, 0, w - kw)
    col_ok = (kc >= cstart) & (kc < cstart + kw)
    dc = np.clip(kc - qc + kw - 1, 0, 2 * kw - 2)
    per_dr = jnp.where(col_ok[None, None], rpb[:, :, dc], NEG_INF)
    r0s = np.array([0, qr, rows - qr])
    k0s = np.clip(r0s - kh // 2, 0, rows - kr)
    qrow = r0s[:, None, None] + np.arange(qr)[None, :, None]
    krow = k0s[:, None, None] + np.arange(kr)[None, None, :]
    win0 = np.clip(qrow - kh // 2, 0, rows - kh)
    row_ok = (krow >= win0) & (krow < win0 + kh)
    dr = np.clip(krow - qrow + kh - 1, 0, 2 * kh - 2)
    full = per_dr[:, dr]
    full = jnp.where(row_ok[None, :, :, :, None, None], full, NEG_INF)
    full = jnp.transpose(full, (0, 1, 2, 4, 3, 5))
    return full.reshape(rpb.shape[0], 3, qr * w, kr * w).astype(F32)


def _neighbourhood(p0, rpb, b, s, c_len, hd, col0, *, nsub=8):
    nh, w, qr = NA_HEADS, GRID_W, NA_QROWS
    rows = s // w
    kr = qr + NA_KH - 1
    assert rows >= kr and rows % qr == 0 and hd == LANE and s % c_len == 0
    nblk = rows // qr
    nsub = int(np.gcd(nsub, nblk))
    nstep = nblk // nsub
    cb0 = (b * s) // c_len
    qo, ko, vo = col0, col0 + nh, col0 + 2 * nh
    bias = _na_bias(rpb, rows)
    lat = lambda off: pl.BlockSpec((s, hd), lambda i, h, k, off=off: (i, off + h))
    ctx = lambda off: pl.BlockSpec((c_len, hd), lambda i, h, k, off=off: (cb0 + i, off + h))
    return pl.pallas_call(
        functools.partial(_na_kernel, qrows=qr, krows=kr, rows=rows, scale=hd ** -0.5, nsub=nsub),
        out_shape=[jax.ShapeDtypeStruct((b * s, nh * hd), BF16),
                   jax.ShapeDtypeStruct((b * c_len, nh * hd), BF16)],
        grid=(b, nh, nstep),
        in_specs=[pl.BlockSpec((nsub * qr * w, hd), lambda i, h, k: (i * nstep + k, qo + h)),
                  lat(ko), lat(vo), ctx(qo), ctx(ko), ctx(vo),
                  pl.BlockSpec((1, 3, qr * w, kr * w), lambda i, h, k: (h, 0, 0, 0))],
        out_specs=[pl.BlockSpec((nsub * qr * w, hd), lambda i, h, k: (i * nstep + k, h)),
                   pl.BlockSpec((c_len, hd), lambda i, h, k: (i, h))],
        compiler_params=_cparams(("parallel", "parallel", "arbitrary")),
        name="neighbourhood",
    )(p0, p0, p0, p0, p0, p0, bias)


def _kpe_kernel(x_ref, t_ref, o_ref, *, n_lat):
    x = x_ref[...].astype(F32)
    half = LANE // 2
    lane = lax.broadcasted_iota(jnp.int32, x.shape, 1)
    r = x * t_ref[...]
    roped = r + pltpu.roll(r, half, 1)
    is_lat = pl.program_id(0) < n_lat
    o_ref[...] = jnp.where(lane < half, jnp.where(is_lat, roped, x), 0.0).astype(o_ref.dtype)


def _mla_kernel(qn_ref, qp_ref, tq_ref, kn_ref, kp_ref, v_ref, knc_ref, kpc_ref, vc_ref, o_ref, k_s, *, scale, tk):
    c_len, s_len = knc_ref.shape[0], kn_ref.shape[0]
    tq = qn_ref.shape[0]
    half = LANE // 2

    @pl.when(pl.program_id(2) == 0)
    def _():
        k_s[0:c_len, 0:LANE] = knc_ref[...]
        k_s[0:c_len, LANE:] = kpc_ref[...]
        k_s[c_len:, 0:LANE] = kn_ref[...]
        k_s[c_len:, LANE:] = kp_ref[...]

    c = scale * np.log2(np.e)
    r = qp_ref[...].astype(F32) * tq_ref[...]
    qpe = r + pltpu.roll(r, half, 1)
    q = jnp.concatenate([(qn_ref[...].astype(F32) * c).astype(BF16), (qpe * c).astype(BF16)], axis=-1)

    def update(k, v, carry):
        m, l, acc = carry
        s = _dot_nt(q, k)
        m_new = jnp.maximum(m, jnp.max(s, axis=-1, keepdims=True))
        alpha = jnp.exp2(m - m_new)
        p = jnp.exp2(s - jnp.tile(m_new, (1, s.shape[1] // LANE)))
        l = alpha * l + jnp.sum(p, axis=-1, keepdims=True)
        acc = alpha * acc + _dot(p.astype(BF16), v)
        return m_new, l, acc

    carry = (jnp.full((tq, LANE), NEG_INF, F32), jnp.zeros((tq, LANE), F32), jnp.zeros((tq, LANE), F32))
    carry = update(k_s[0:c_len, :], vc_ref[...], carry)
    for j in range(s_len // tk):
        carry = update(k_s[c_len + j * tk: c_len + (j + 1) * tk, :], v_ref[j * tk:(j + 1) * tk, :], carry)
    _, l, acc = carry
    o_ref[...] = (acc / l).astype(o_ref.dtype)


def _mla_attention(q, kv, kpe, tab, b, s, c_len, *, tq=512, tk=2048):
    nh = MLA_HEADS
    assert MLA_NOPE == LANE and MLA_V == LANE and 2 * MLA_ROPE == LANE and c_len % LANE == 0
    tq, tk = _tile(s, tq, SUBLANE), _tile(s, tk)
    nq = s // tq
    cb0 = (b * s) // c_len
    return pl.pallas_call(
        functools.partial(_mla_kernel, scale=(MLA_NOPE + MLA_ROPE) ** -0.5, tk=tk),
        out_shape=jax.ShapeDtypeStruct((b * s, nh * MLA_V), BF16),
        grid=(b, nh, nq),
        in_specs=[pl.BlockSpec((tq, LANE), lambda i, h, a: (i * nq + a, h)),
                  pl.BlockSpec((tq, LANE), lambda i, h, a: (i * nq + a, nh + h)),
                  pl.BlockSpec((tq, LANE), lambda i, h, a: (a, 0)),
                  pl.BlockSpec((s, LANE), lambda i, h, a: (i, h)),
                  pl.BlockSpec((s, LANE), lambda i, h, a: (i, 0)),
                  pl.BlockSpec((s, LANE), lambda i, h, a: (i, nh + h)),
                  pl.BlockSpec((c_len, LANE), lambda i, h, a: (cb0 + i, h)),
                  pl.BlockSpec((c_len, LANE), lambda i, h, a: (cb0 + i, 0)),
                  pl.BlockSpec((c_len, LANE), lambda i, h, a: (cb0 + i, nh + h))],
        out_specs=pl.BlockSpec((tq, MLA_V), lambda i, h, a: (i * nq + a, h)),
        scratch_shapes=[pltpu.VMEM((c_len + s, 2 * LANE), BF16)],
        compiler_params=_cparams(("parallel", "parallel", "arbitrary")),
        name="mla_attention",
    )(q, q, tab, kv, kpe, kv, kv, kpe, kv)


def _lru_kernel(xl_ref, xc_ref, yl_ref, cw_ref, cb_ref, wa_ref, wi_ref, ba_ref, bi_ref, lam_ref, o_ref,
                xp_l, xp_c, a_f, u_f, a_b, u_b, *, rows_per_step):
    s_len, c_len = xl_ref.shape[0], xc_ref.shape[0]
    pad = SUBLANE
    left = (LRU_CONV - 1) // 2
    cw, cb = cw_ref[...], cb_ref[...]

    def softplus(x):
        return jnp.maximum(x, 0.0) + jnp.log1p(jnp.exp(-jnp.abs(x)))

    sp = [softplus(-lam_ref[d:d + 1, :]) for d in range(2)]

    def gates(xp, n, f_off, b_off):
        step = min(rows_per_step, n)
        for r0 in range(0, n, step):
            x = cb
            for j in range(LRU_CONV):
                x = x + cw[j:j + 1, :] * xp[pad - left + j + r0: pad - left + j + r0 + step, :]
            xb = x.astype(BF16)

            def blockdiag(w_ref, d):
                return jnp.concatenate([_dot(xb[:, g * LANE:(g + 1) * LANE], w_ref[d, g])
                                        for g in range(w_ref.shape[1])], axis=-1)

            for d, (a_ref, u_ref, off) in enumerate(((a_f, u_f, f_off), (a_b, u_b, b_off))):
                rg = _sigmoid(blockdiag(wa_ref, d) + ba_ref[d:d + 1, :])
                ig = _sigmoid(blockdiag(wi_ref, d) + bi_ref[d:d + 1, :])
                log_a = -LRU_C * rg * sp[d]
                a_ref[off + r0: off + r0 + step, :] = jnp.exp(log_a)
                th = jnp.tanh(log_a)
                u_ref[off + r0: off + r0 + step, :] = jnp.sqrt(-2.0 * th / (1.0 - th)) * ig * x

    zeros = jnp.zeros((pad, xl_ref.shape[1]), F32)
    for xp, src, n in ((xp_l, xl_ref, s_len), (xp_c, xc_ref, c_len)):
        xp[0:pad, :] = zeros
        xp[pad + n: 2 * pad + n, :] = zeros
        xp[pad: pad + n, :] = src[...].astype(F32)
    gates(xp_c, c_len, 0, s_len)
    gates(xp_l, s_len, c_len, 0)

    n_tiles = (s_len + c_len) // SUBLANE
    row = lax.broadcasted_iota(jnp.int32, (SUBLANE, xl_ref.shape[1]), 0)

    def body(i, carry):
        hf, hb = carry
        rf = pl.ds(pl.multiple_of(i * SUBLANE, SUBLANE), SUBLANE)
        a, u = a_f[rf, :], u_f[rf, :]
        for sft in (1, 2, 4):
            m = row >= sft
            u = jnp.where(m, a * pltpu.roll(u, sft, 0) + u, u)
            a = jnp.where(m, a * pltpu.roll(a, sft, 0), a)
        h = a * hf + u
        u_f[rf, :] = h
        hf = h[SUBLANE - 1:SUBLANE, :]
        rb = pl.ds(pl.multiple_of((n_tiles - 1 - i) * SUBLANE, SUBLANE), SUBLANE)
        a, u = a_b[rb, :], u_b[rb, :]
        for sft in (1, 2, 4):
            m = row < SUBLANE - sft
            u = jnp.where(m, a * pltpu.roll(u, SUBLANE - sft, 0) + u, u)
            a = jnp.where(m, a * pltpu.roll(a, SUBLANE - sft, 0), a)
        h = a * hb + u
        u_b[rb, :] = h
        hb = h[0:1, :]
        return hf, hb

    h0 = jnp.zeros((1, xl_ref.shape[1]), F32)
    lax.fori_loop(0, n_tiles, body, (h0, h0), unroll=4)

    step = min(rows_per_step, s_len)
    for r0 in range(0, s_len, step):
        y = yl_ref[r0:r0 + step, :].astype(F32)
        h = u_f[c_len + r0: c_len + r0 + step, :] + u_b[r0:r0 + step, :]
        o_ref[r0:r0 + step, :] = (h * jax.nn.gelu(y)).astype(o_ref.dtype)


def _rglru(p1, conv_w, conv_b, wa, wi, ba, bi, lam, b, s, c_len, x_col0, y_col0, *, group=2):
    nblk, bd = wa.shape[1], wa.shape[2]
    width = nblk * bd
    group = int(np.gcd(np.gcd(group, nblk), np.gcd(x_col0, y_col0)))
    gw = group * bd
    assert bd == LANE and s % c_len == 0 and (s + c_len) % (4 * SUBLANE) == 0
    cb0 = (b * s) // c_len
    t = s + c_len
    xc, yc = x_col0 // group, y_col0 // group
    vec2 = pl.BlockSpec((2, gw), lambda i, k: (0, k))
    wspec = pl.BlockSpec((2, group, bd, bd), lambda i, k: (0, k, 0, 0))
    return pl.pallas_call(
        functools.partial(_lru_kernel, rows_per_step=512),
        out_shape=jax.ShapeDtypeStruct((b * s, width), BF16),
        grid=(b, nblk // group),
        in_specs=[pl.BlockSpec((s, gw), lambda i, k: (i, xc + k)),
                  pl.BlockSpec((c_len, gw), lambda i, k: (cb0 + i, xc + k)),
                  pl.BlockSpec((s, gw), lambda i, k: (i, yc + k)),
                  pl.BlockSpec((LRU_CONV, gw), lambda i, k: (0, k)),
                  pl.BlockSpec((1, gw), lambda i, k: (0, k)),
                  wspec, wspec, vec2, vec2, vec2],
        out_specs=pl.BlockSpec((s, gw), lambda i, k: (i, k)),
        scratch_shapes=[pltpu.VMEM((s + 2 * SUBLANE, gw), F32), pltpu.VMEM((c_len + 2 * SUBLANE, gw), F32),
                        pltpu.VMEM((t, gw), F32), pltpu.VMEM((t, gw), F32),
                        pltpu.VMEM((t, gw), F32), pltpu.VMEM((t, gw), F32)],
        compiler_params=_cparams(("parallel", "parallel")),
        name="rglru",
    )(p1, p1, p1, conv_w.astype(F32), conv_b.reshape(1, width).astype(F32),
      wa.astype(BF16), wi.astype(BF16), ba.astype(F32), bi.astype(F32), lam.astype(F32))


def _rope_angles(n, rot_dim):
    t = np.arange(n)
    row = (t // GRID_W).astype(np.float32)
    col = (t % GRID_W).astype(np.float32)
    nf = rot_dim // 4
    inv = jnp.asarray(ROPE_BASE, F32) ** (-jnp.arange(nf, dtype=F32) / nf)
    return row[:, None] * inv, col[:, None] * inv


def _swap_perm(rot_dim):
    q = rot_dim // 4
    idx = np.arange(rot_dim).reshape(4, q)
    return idx[[1, 0, 3, 2]].reshape(-1)


def kernel(x, c, ctx, c_ctx, w_mod, b_mod, norm_g, w_in_even, w_out_even, ret_decay, na_rpb, ffn_w_gate, ffn_w_up,
           ffn_w_down, w_in_odd, mla_q_norm, mla_w_uq, mla_kv_norm, mla_w_ukv, lru_conv_w, lru_conv_b, lru_w_a,
           lru_b_a, lru_w_i, lru_b_i, lru_lambda, w_out_odd, router_w, moe_w_gate, moe_w_up, moe_w_down):
    b, s, d = x.shape
    c_len = ctx.shape[1]
    n_lat, n_ctx = b * s, b * c_len
    n_tok = n_lat + n_ctx
    depth = w_mod.shape[0]
    assert depth == 2 and w_in_even.shape[0] == 1 and w_in_odd.shape[0] == 1
    assert n_ctx % SUBLANE == 0 and s % c_len == 0

    mod_rows = -(-(b + 1) // SUBLANE) * SUBLANE
    s_in = jnp.concatenate([c, c_ctx[None], jnp.zeros((mod_rows - b - 1, d), F32)], axis=0)
    mods = _modulation(s_in, w_mod, b_mod).reshape(depth, mod_rows, 6, 1, d)
    mod = lambda l, j: mods[l, :, j]
    mod_row = lambda r0: jnp.where(r0 < n_lat, r0 // s, b)


    g = norm_g[0]
    (h,) = _resnorm(x, n_tok, mod_row, x_ctx=ctx, gn=g[0], sc=mod(0, 0), sh=mod(0, 1))
    w_in = w_in_even[0].astype(BF16)
    p0 = _mm([(h, d, 0)], [(w_in, d, 0)], n_tok, w_in.shape[1], BF16, name="in_proj_even")
    ret_w = w_out_even.shape[1] - NA_HEADS * LANE
    dk = ret_w // RET_HEADS
    ar, ac = _rope_angles(s, dk)
    rope_a = jnp.concatenate([jnp.cos(ar), jnp.cos(ac)], axis=-1)
    rope_b = jnp.concatenate([jnp.sin(ar), jnp.sin(ac)], axis=-1)
    ret_l, ret_c = _retention(p0, ret_decay[0], rope_a, rope_b, b, s, c_len, dk, dk)
    na_l, na_c = _neighbourhood(p0, na_rpb[0], b, s, c_len, LANE, 4 * ret_w // LANE)
    mix_a = jnp.concatenate([ret_l, ret_c], axis=0)
    mix_b = jnp.concatenate([na_l, na_c], axis=0)
    w_out = w_out_even[0].astype(BF16)
    half_k = w_out.shape[0] // 2
    assert mix_a.shape[1] == half_k and mix_b.shape[1] == half_k
    o = _mm([(mix_a, half_k, 0), (mix_b, half_k, 0)], [(w_out, half_k, 0), (w_out, half_k, 1)],
            n_tok, d, BF16, name="out_proj_even")
    xs, h = _resnorm(x, n_tok, mod_row, x_ctx=ctx, y=o, gate=mod(0, 2), gy=g[1], gn=g[2], sc=mod(0, 3),
                     sh=mod(0, 4))
    y = _ffn(h, ffn_w_gate[0].astype(BF16), ffn_w_up[0].astype(BF16), ffn_w_down[0].astype(BF16))
    g1 = norm_g[1]
    xs, h = _resnorm(xs, n_tok, mod_row, y=y, gate=mod(0, 5), gy=g[3], gn=g1[0], sc=mod(1, 0), sh=mod(1, 1))

    g = g1
    qr, kvr, rr = MLA_Q_RANK, MLA_KV_RANK, MLA_ROPE
    lw = lru_w_a.shape[2] * lru_w_a.shape[3]
    perm = _swap_perm(rr)
    wi = w_in_odd[0]
    o_q, o_kv, o_kr, o_lx, o_ly = 0, qr, qr + kvr, qr + kvr + rr, qr + kvr + rr + lw
    kr_w = wi[:, o_kr:o_kr + rr]
    w_in = jnp.concatenate([wi[:, o_q:o_q + qr], wi[:, o_kv:o_kv + kvr], wi[:, o_lx:o_lx + lw],
                            wi[:, o_ly:o_ly + lw], kr_w, kr_w[:, perm]], axis=-1).astype(BF16)
    assert qr % LANE == 0 and kvr % LANE == 0 and lw % LANE == 0 and 2 * rr == LANE
    p1 = _mm([(h, d, 0)], [(w_in, d, 0)], n_tok, w_in.shape[1], BF16, tn=896, name="in_proj_odd")
    nh = MLA_HEADS
    wq = mla_w_uq[0].reshape(qr, nh, MLA_NOPE + rr)
    wq_pe = wq[:, :, MLA_NOPE:]
    wq = jnp.concatenate([wq[:, :, :MLA_NOPE].reshape(qr, nh * MLA_NOPE),
                          jnp.concatenate([wq_pe, wq_pe[:, :, perm]], axis=-1).reshape(qr, nh * LANE)],
                         axis=-1).astype(BF16)
    wkv = mla_w_ukv[0].reshape(kvr, nh, MLA_NOPE + MLA_V)
    wkv = jnp.concatenate([wkv[:, :, :MLA_NOPE].reshape(kvr, nh * MLA_NOPE),
                           wkv[:, :, MLA_NOPE:].reshape(kvr, nh * MLA_V)], axis=-1).astype(BF16)
    assert qr % kvr == 0
    q = _mm([(p1, qr, 0)], [(wq, qr, 0)], n_lat, wq.shape[1], BF16, norm_g=mla_q_norm[0], name="mla_q")
    kv = _mm([(p1, kvr, qr // kvr)], [(wkv, kvr, 0)], n_tok, wkv.shape[1], BF16, norm_g=mla_kv_norm[0],
             name="mla_kv")
    ar, ac = _rope_angles(s, rr)
    cos4 = jnp.concatenate([jnp.cos(ar), jnp.cos(ar), jnp.cos(ac), jnp.cos(ac)], axis=-1)
    sin4 = jnp.concatenate([-jnp.sin(ar), jnp.sin(ar), -jnp.sin(ac), jnp.sin(ac)], axis=-1)
    tab = jnp.concatenate([cos4, sin4], axis=-1)
    kr_cb = (qr + kvr + 2 * lw) // LANE
    tr = _tile(s, 512, SUBLANE)
    assert n_ctx % tr == 0 or tr % n_ctx == 0
    tr = min(tr, n_ctx) if n_ctx % tr else tr
    nt_lat, nt_seq = n_lat // tr, s // tr
    kpe = pl.pallas_call(
        functools.partial(_kpe_kernel, n_lat=nt_lat),
        out_shape=jax.ShapeDtypeStruct((n_tok, LANE), BF16),
        grid=(n_tok // tr,),
        in_specs=[pl.BlockSpec((tr, LANE), lambda m: (m, kr_cb)),
                  pl.BlockSpec((tr, LANE), lambda m: (jnp.where(m < nt_lat, m % nt_seq, 0), 0))],
        out_specs=pl.BlockSpec((tr, LANE), lambda m: (m, 0)),
        compiler_params=_cparams(("parallel",)),
        name="mla_kpe",
    )(p1, tab)
    att = _mla_attention(q, kv, kpe, tab, b, s, c_len)
    lru = _rglru(p1, lru_conv_w[0], lru_conv_b[0], lru_w_a[0], lru_w_i[0], lru_b_a[0], lru_b_i[0], lru_lambda[0],
                 b, s, c_len, (qr + kvr) // LANE, (qr + kvr + lw) // LANE)
    w_out = w_out_odd[0].astype(BF16)
    half_k = w_out.shape[0] // 2
    assert att.shape[1] == half_k and lru.shape[1] == half_k
    o = _mm([(att, half_k, 0), (lru, half_k, 0)], [(w_out, half_k, 0), (w_out, half_k, 1)],
            n_lat, d, BF16, name="out_proj_odd")
    xl, hpk, route = _resnorm(xs, n_lat, mod_row, y=o, gate=mod(1, 2), gy=g[1], gn=g[2], sc=mod(1, 3),
                              sh=mod(1, 4), router_w=router_w[0])
    ypk = _moe(hpk, route, moe_w_gate[0].astype(BF16), moe_w_up[0].astype(BF16), moe_w_down[0].astype(BF16))
    xl = _moe_finish(xl, ypk, route, mod_row, mod(1, 5), g[3])
    return xl.reshape(b, s, d)
```

```python
import functools

import numpy as np
import jax
import jax.numpy as jnp
from jax import lax
from jax.experimental import pallas as pl
from jax.experimental.pallas import tpu as pltpu

F32 = jnp.float32
BF16 = jnp.bfloat16

EPS = 1e-6
NEG_INF = -1e30
ROPE_BASE = 10000.0
GRID_W = 64
RET_HEADS = 8
RET_CHUNK = 128
NA_HEADS = 16
NA_KH = 8
NA_KW = 16
NA_QROWS = 8
MLA_HEADS = 16
MLA_Q_RANK = 1536
MLA_KV_RANK = 512
MLA_NOPE = 128
MLA_ROPE = 64
MLA_V = 128
LRU_C = 8.0
LRU_CONV = 4
TOP_K = 2

LANE = 128
SUBLANE = 8
VMEM_LIMIT = 52 * 1024 * 1024


def _cparams(sem):
    return pltpu.CompilerParams(dimension_semantics=sem, vmem_limit_bytes=VMEM_LIMIT)


def _tile(n, target, mult=LANE):
    if n <= target:
        return n
    best = None
    for t in range(mult, target + 1, mult):
        if n % t == 0:
            best = t
    assert best is not None, (n, target, mult)
    return best


def _dot(a, b):
    return jnp.dot(a, b, preferred_element_type=F32)


def _dot_nt(a, b):
    return lax.dot_general(a, b, (((1,), (1,)), ((), ())), preferred_element_type=F32)


def _dot_tn(a, b):
    return lax.dot_general(a, b, (((0,), (0,)), ((), ())), preferred_element_type=F32)


def _sigmoid(x):
    return 0.5 * jnp.tanh(0.5 * x) + 0.5


def _rms(x, g):
    return x * lax.rsqrt(jnp.mean(x * x, axis=-1, keepdims=True) + EPS) * g


def _mod_kernel(s_ref, w_ref, b_ref, o_ref):
    s = s_ref[...]
    s = (s * jax.nn.sigmoid(s)).astype(BF16)
    o_ref[0] = _dot(s, w_ref[0].astype(BF16)) + b_ref[0]


def _modulation(s_in, w_mod, b_mod):
    nl, d, n = w_mod.shape
    rows = s_in.shape[0]
    tn = _tile(n, 512)
    return pl.pallas_call(
        _mod_kernel,
        out_shape=jax.ShapeDtypeStruct((nl, rows, n), F32),
        grid=(nl, n // tn),
        in_specs=[pl.BlockSpec((rows, d), lambda l, j: (0, 0)),
                  pl.BlockSpec((1, d, tn), lambda l, j: (l, 0, j)),
                  pl.BlockSpec((1, 1, tn), lambda l, j: (l, 0, j))],
        out_specs=pl.BlockSpec((1, rows, tn), lambda l, j: (l, 0, j)),
        compiler_params=_cparams(("arbitrary", "arbitrary")),
        name="modulation",
    )(s_in, w_mod, b_mod.reshape(nl, 1, n))


def _pack_halves(v):
    half = v.shape[1] // 2
    lo = lax.bitcast_convert_type(v[:, :half].astype(BF16).astype(F32), jnp.uint32) >> 16
    hi = lax.bitcast_convert_type(v[:, half:].astype(BF16).astype(F32), jnp.uint32) & jnp.uint32(0xFFFF0000)
    return lo | hi


def _unpack_halves(u):
    lo = lax.bitcast_convert_type(u << 16, F32)
    hi = lax.bitcast_convert_type(u & jnp.uint32(0xFFFF0000), F32)
    return jnp.concatenate([lo, hi], axis=-1)


def _route_top2(logits, n_exp):
    lane = lax.broadcasted_iota(jnp.int32, logits.shape, 1)
    big = jnp.int32(2 ** 30)
    logits = jnp.where(lane < n_exp, logits, -jnp.inf)
    m1 = jnp.max(logits, axis=-1, keepdims=True)
    i1 = jnp.min(jnp.where(logits == m1, lane, big), axis=-1, keepdims=True)
    rest = jnp.where(lane == i1, -jnp.inf, logits)
    m2 = jnp.max(rest, axis=-1, keepdims=True)
    i2 = jnp.min(jnp.where(rest == m2, lane, big), axis=-1, keepdims=True)
    e2 = jnp.exp(m2 - m1)
    g1 = 1.0 / (1.0 + e2)
    g2 = e2 / (1.0 + e2)
    out = jnp.where(lane == 0, i1.astype(F32), 0.0) + jnp.where(lane == 1, i2.astype(F32), 0.0)
    return out + jnp.where(lane == 2, g1, 0.0) + jnp.where(lane == 3, g2, 0.0)


def _resnorm_kernel(*refs, has_y, has_next, n_exp, lat_tiles):
    it = iter(refs)
    x_ref = next(it)
    xc_ref = next(it) if lat_tiles else None
    if has_y:
        y_ref, gate_ref, gy_ref = next(it), next(it), next(it)
    if has_next:
        gn_ref, sc_ref, sh_ref = next(it), next(it), next(it)
    if n_exp:
        rw_ref = next(it)
    if lat_tiles:
        x = jnp.where(pl.program_id(0) < lat_tiles, x_ref[0], xc_ref[0])
    else:
        x = x_ref[...]
    if has_y:
        xo_ref = next(it)
        x = x + gate_ref[0] * _rms(y_ref[...].astype(F32), gy_ref[...])
        xo_ref[...] = x
    if has_next:
        h_ref = next(it)
        h = _rms(x, gn_ref[...]) * (1.0 + sc_ref[0]) + sh_ref[0]
        if n_exp:
            h_ref[...] = _pack_halves(h)
            route_ref = next(it)
            logits = jnp.dot(h, rw_ref[...], preferred_element_type=F32, precision=lax.Precision.HIGHEST)
            route_ref[...] = _route_top2(logits, n_exp)
        else:
            h_ref[...] = h.astype(BF16)


def _resnorm(x, nrows, mod_row, *, x_ctx=None, y=None, y_batch_major=False, gate=None, gy=None, gn=None, sc=None,
             sh=None, router_w=None, tr=256):
    d = x.shape[-1]
    tr = _tile(nrows, tr, SUBLANE)
    has_y, has_next = y is not None, gn is not None
    n_exp = 0 if router_w is None else router_w.shape[1]
    row = pl.BlockSpec((tr, d), lambda m: (m, 0))
    vec = pl.BlockSpec((1, d), lambda m: (0, 0))
    modv = pl.BlockSpec((1, 1, d), lambda m: (mod_row(m * tr), 0, 0))
    args, specs, outs, ospecs = [x], [row], [], []
    lat_tiles = 0
    if x_ctx is not None:
        (_, s_len, _), c_len = x.shape, x_ctx.shape[1]
        tr = _tile(int(np.gcd(s_len, c_len)), tr, SUBLANE)
        spt, cpt = s_len // tr, c_len // tr
        lat_tiles = x.shape[0] * spt
        lat = lambda m: jnp.minimum(m, lat_tiles - 1)
        cx = lambda m: jnp.maximum(m - lat_tiles, 0)
        args = [x, x_ctx]
        specs = [pl.BlockSpec((1, tr, d), lambda m: (lat(m) // spt, lat(m) % spt, 0)),
                 pl.BlockSpec((1, tr, d), lambda m: (cx(m) // cpt, cx(m) % cpt, 0))]
        row = pl.BlockSpec((tr, d), lambda m: (m, 0))
        modv = pl.BlockSpec((1, 1, d), lambda m: (mod_row(m * tr), 0, 0))
    yrow = row
    if y_batch_major:
        per = spt + cpt
        yrow = pl.BlockSpec((tr, d), lambda m: (jnp.where(m < lat_tiles, (m // spt) * per + cpt + m % spt,
                                                          (cx(m) // cpt) * per + cx(m) % cpt), 0))
    if has_y:
        args += [y, gate, gy.reshape(1, d)]
        specs += [yrow, modv, vec]
        outs.append(jax.ShapeDtypeStruct((nrows, d), F32))
        ospecs.append(row)
    if has_next:
        args += [gn.reshape(1, d), sc, sh]
        specs += [vec, modv, modv]
        if n_exp:
            assert n_exp <= LANE and TOP_K == 2
            args.append(jnp.pad(router_w.astype(F32), ((0, 0), (0, LANE - n_exp))))
            specs.append(pl.BlockSpec((d, LANE), lambda m: (0, 0)))
            outs += [jax.ShapeDtypeStruct((nrows, d // 2), jnp.uint32), jax.ShapeDtypeStruct((nrows, LANE), F32)]
            ospecs += [pl.BlockSpec((tr, d // 2), lambda m: (m, 0)), pl.BlockSpec((tr, LANE), lambda m: (m, 0))]
        else:
            outs.append(jax.ShapeDtypeStruct((nrows, d), BF16))
            ospecs.append(row)
    res = pl.pallas_call(
        functools.partial(_resnorm_kernel, has_y=has_y, has_next=has_next, n_exp=n_exp, lat_tiles=lat_tiles),
        out_shape=outs, grid=(nrows // tr,), in_specs=specs, out_specs=ospecs,
        compiler_params=_cparams(("parallel",)),
        name="resnorm",
    )(*args)
    return res


def _mm_kernel(*refs, nparts, has_norm):
    x_refs, w_refs = refs[:nparts], refs[nparts:2 * nparts]
    pos = 2 * nparts
    if has_norm:
        g_ref, o_ref, h_ref = refs[pos], refs[pos + 1], refs[pos + 2]

        @pl.when(pl.program_id(1) == 0)
        def _():
            h_ref[...] = _rms(x_refs[0][...].astype(F32), g_ref[...]).astype(BF16)

        acc = _dot(h_ref[...], w_refs[0][...])
    else:
        o_ref = refs[pos]
        acc = _dot(x_refs[0][...], w_refs[0][...])
        for i in range(1, nparts):
            acc = acc + _dot(x_refs[i][...], w_refs[i][...])
    o_ref[...] = acc.astype(o_ref.dtype)


def _mm(xs, ws, m, n, out_dtype, *, norm_g=None, tm=1024, tn=512, name="mm"):
    tm, tn = _tile(m, tm, SUBLANE), _tile(n, tn)
    has_norm = norm_g is not None
    args, specs = [], []
    for (a, k, cb) in xs:
        args.append(a)
        specs.append(pl.BlockSpec((tm, k), lambda i, j, cb=cb: (i, cb)))
    for (w, k, rb) in ws:
        args.append(w)
        specs.append(pl.BlockSpec((k, tn), lambda i, j, rb=rb: (rb, j)))
    scratch = []
    if has_norm:
        k0 = xs[0][1]
        args.append(norm_g.reshape(1, k0).astype(F32))
        specs.append(pl.BlockSpec((1, k0), lambda i, j: (0, 0)))
        scratch.append(pltpu.VMEM((tm, k0), BF16))
    return pl.pallas_call(
        functools.partial(_mm_kernel, nparts=len(xs), has_norm=has_norm),
        out_shape=jax.ShapeDtypeStruct((m, n), out_dtype),
        grid=(m // tm, n // tn), in_specs=specs,
        out_specs=pl.BlockSpec((tm, tn), lambda i, j: (i, j)),
        scratch_shapes=scratch,
        compiler_params=_cparams(("parallel", "arbitrary")),
        name=name,
    )(*args)


def _swiglu_step(h, wg, wu, wd):
    g = _dot(h, wg)
    u = _dot(h, wu)
    return _dot((g * jax.nn.sigmoid(g) * u).astype(BF16), wd)


def _ffn_kernel(h_ref, wg_ref, wu_ref, wd_ref, o_ref):
    @pl.when(pl.program_id(1) == 0)
    def _():
        o_ref[...] = jnp.zeros_like(o_ref)

    o_ref[...] += _swiglu_step(h_ref[...], wg_ref[...], wu_ref[...], wd_ref[...])


def _ffn(h, wg, wu, wd, *, tm=512, tf=256):
    m, d = h.shape
    f = wg.shape[1]
    tm, tf = _tile(m, tm, SUBLANE), _tile(f, tf)
    return pl.pallas_call(
        _ffn_kernel,
        out_shape=jax.ShapeDtypeStruct((m, d), F32),
        grid=(m // tm, f // tf),
        in_specs=[pl.BlockSpec((tm, d), lambda i, j: (i, 0)),
                  pl.BlockSpec((d, tf), lambda i, j: (0, j)),
                  pl.BlockSpec((d, tf), lambda i, j: (0, j)),
                  pl.BlockSpec((tf, d), lambda i, j: (j, 0))],
        out_specs=pl.BlockSpec((tm, d), lambda i, j: (i, 0)),
        compiler_params=_cparams(("parallel", "arbitrary")),
        name="ffn",
    )(h, wg, wu, wd)


def _moe_kernel(te_ref, nu_ref, src_ref, aid_ref, h_hbm, wg_ref, wu_ref, wd_ref, y_hbm,
                hbuf, hb, acc, stage, gsem, ssem, *, nf, p_rows):
    i, f = pl.program_id(0), pl.program_id(1)
    nt = pl.num_programs(0)
    tm = hb.shape[0]
    rows = tm // nf
    n_used = nu_ref[0]
    used = i < n_used
    slot = i % 2
    r0 = f * rows

    def gather_row(tile, sl, r):
        tok = src_ref[tile * tm + r]
        pltpu.make_async_copy(h_hbm.at[pl.ds(tok, 1)], hbuf.at[sl, pl.ds(r, 1)], gsem.at[sl]).start()

    def scatter_row(sl, r, dst):
        pltpu.make_async_copy(stage.at[sl, pl.ds(r, 1)], y_hbm.at[pl.ds(dst, 1)], ssem.at[sl]).start()

    def tile_done(buf, sem):
        pltpu.make_async_copy(buf, buf, sem).wait()

    def scatter_slice():
        prev = jnp.maximum(i - 1, 0)
        for r in range(rows):
            dst = jnp.where(i >= 1, aid_ref[prev * tm + r0 + r], p_rows + r0 + r)
            scatter_row(1 - slot, r0 + r, dst)

    @pl.when(jnp.logical_and(f == 0, i == 0))
    def _():
        stage[1] = jnp.zeros(stage.shape[1:], stage.dtype)

        def body(r, _):
            gather_row(0, 0, r)
            return 0
        lax.fori_loop(0, tm, body, 0, unroll=8)

    @pl.when(jnp.logical_and(f == 0, i <= n_used))
    def _():
        tile_done(hbuf.at[slot], gsem.at[slot])

    @pl.when(jnp.logical_and(f == 0, used))
    def _():
        hb[...] = _unpack_halves(hbuf[slot]).astype(BF16)
        acc[...] = jnp.zeros_like(acc)

    @pl.when(used)
    def _():
        for r in range(rows):
            gather_row(i + 1, 1 - slot, r0 + r)
        scatter_slice()
        acc[...] += _swiglu_step(hb[...], wg_ref[0], wu_ref[0], wd_ref[0])

    @pl.when(jnp.logical_not(used))
    def _():
        scatter_slice()

    @pl.when(f == nf - 1)
    def _():
        @pl.when(i >= 1)
        def _():
            tile_done(stage.at[slot], ssem.at[slot])

        @pl.when(used)
        def _():
            stage[slot] = _pack_halves(acc[...])

        @pl.when(jnp.logical_not(used))
        def _():
            stage[slot] = jnp.zeros(stage.shape[1:], stage.dtype)

        @pl.when(i == nt - 1)
        def _():
            def body(r, _):
                scatter_row(slot, r, aid_ref[i * tm + r])
                return 0
            lax.fori_loop(0, tm, body, 0, unroll=8)
            tile_done(stage.at[slot], ssem.at[slot])
            tile_done(stage.at[1 - slot], ssem.at[1 - slot])


def _moe_route(route, n_exp, tm):
    n = route.shape[0]
    e = route[:, :TOP_K].astype(jnp.int32).T.reshape(-1)
    onehot = (e[:, None] == jnp.arange(n_exp, dtype=jnp.int32)[None, :]).astype(jnp.int32)
    rank = jnp.sum((jnp.cumsum(onehot, axis=0) - onehot) * onehot, axis=1)
    counts = jnp.sum(onehot, axis=0)
    padded = ((counts + tm - 1) // tm) * tm
    ends = jnp.cumsum(padded)
    starts = ends - padded
    dest = starts[e] + rank
    p_rows = TOP_K * n + n_exp * tm
    n_tiles = p_rows // tm
    aid = jnp.full((p_rows,), -1, jnp.int32).at[dest].set(jnp.arange(TOP_K * n, dtype=jnp.int32))
    is_pad = aid < 0
    src = jnp.where(is_pad, 0, aid % n)
    aid = jnp.where(is_pad, TOP_K * n + jnp.cumsum(is_pad.astype(jnp.int32)) - 1, aid)
    n_used = (ends[-1] // tm).astype(jnp.int32)
    tile0 = jnp.arange(n_tiles, dtype=jnp.int32) * tm
    te = jnp.minimum(jnp.sum((tile0[:, None] >= ends[None, :]).astype(jnp.int32), axis=1), n_exp - 1)
    te = jnp.where(jnp.arange(n_tiles) < n_used, te, te[jnp.maximum(n_used - 1, 0)])
    return te, n_used.reshape(1), src, aid


def _moe(hpk, route, wg, wu, wd, *, tm=512, tf=256):
    n, dh = hpk.shape
    d = 2 * dh
    n_exp, _, f = wg.shape
    tm, tf = _tile(n, tm, SUBLANE), _tile(f, tf)
    te, n_used, src, aid = _moe_route(route, n_exp, tm)
    p_rows = src.shape[0]
    nf = f // tf
    assert tm % nf == 0
    frozen = lambda i, j, nu: jnp.where(i < nu[0], j, nf - 1)
    grid_spec = pltpu.PrefetchScalarGridSpec(
        num_scalar_prefetch=4,
        grid=(p_rows // tm, nf),
        in_specs=[pl.BlockSpec(memory_space=pl.ANY),
                  pl.BlockSpec((1, d, tf), lambda i, j, te, nu, s, a: (te[i], 0, frozen(i, j, nu))),
                  pl.BlockSpec((1, d, tf), lambda i, j, te, nu, s, a: (te[i], 0, frozen(i, j, nu))),
                  pl.BlockSpec((1, tf, d), lambda i, j, te, nu, s, a: (te[i], frozen(i, j, nu), 0))],
        out_specs=pl.BlockSpec(memory_space=pl.ANY),
        scratch_shapes=[pltpu.VMEM((2, tm, dh), jnp.uint32), pltpu.VMEM((tm, d), BF16), pltpu.VMEM((tm, d), F32),
                        pltpu.VMEM((2, tm, dh), jnp.uint32), pltpu.SemaphoreType.DMA((2,)),
                        pltpu.SemaphoreType.DMA((2,))],
    )
    return pl.pallas_call(
        functools.partial(_moe_kernel, nf=nf, p_rows=p_rows),
        out_shape=jax.ShapeDtypeStruct((p_rows + tm, dh), jnp.uint32),
        grid_spec=grid_spec,
        compiler_params=_cparams(("arbitrary", "arbitrary")),
        name="moe",
    )(te, n_used, src, aid, hpk, wg, wu, wd)


def _moe_finish_kernel(x_ref, y1_ref, y2_ref, r_ref, gate_ref, gy_ref, o_ref):
    r = r_ref[...]
    lane = lax.broadcasted_iota(jnp.int32, r.shape, 1)
    g1 = jnp.sum(jnp.where(lane == 2, r, 0.0), axis=-1, keepdims=True)
    g2 = jnp.sum(jnp.where(lane == 3, r, 0.0), axis=-1, keepdims=True)
    y = g1 * _unpack_halves(y1_ref[...]) + g2 * _unpack_halves(y2_ref[...])
    o_ref[...] = x_ref[...] + gate_ref[0] * _rms(y, gy_ref[...])


def _moe_finish(x, ypk, route, mod_row, gate, gy, *, tr=256):
    n, d = x.shape
    tr = _tile(n, tr, SUBLANE)
    nt = n // tr
    return pl.pallas_call(
        _moe_finish_kernel,
        out_shape=jax.ShapeDtypeStruct((n, d), F32),
        grid=(nt,),
        in_specs=[pl.BlockSpec((tr, d), lambda m: (m, 0)),
                  pl.BlockSpec((tr, d // 2), lambda m: (m, 0)),
                  pl.BlockSpec((tr, d // 2), lambda m: (nt + m, 0)),
                  pl.BlockSpec((tr, LANE), lambda m: (m, 0)),
                  pl.BlockSpec((1, 1, d), lambda m: (mod_row(m * tr), 0, 0)),
                  pl.BlockSpec((1, d), lambda m: (0, 0))],
        out_specs=pl.BlockSpec((tr, d), lambda m: (m, 0)),
        compiler_params=_cparams(("parallel",)),
        name="moe_finish",
    )(x, ypk, ypk, route, gate, gy.reshape(1, d))


def _ret_kernel(dec_ref, ql_ref, kl_ref, vl_ref, gl_ref, qc_ref, kc_ref, vc_ref, gc_ref, ca_ref, sa_ref,
                o_ref, qs, ks, kcs, af_l, ab_l, af_c, ab_c, st_f, st_b, *, chunk, kscale, unroll):
    hd = pl.program_id(1)
    s_len, dk = ql_ref.shape
    c_len = qc_ref.shape[0]
    dv = vl_ref.shape[1]
    ch = chunk
    ncl, ncc = s_len // ch, c_len // ch
    half = LANE // 2

    def sl(c):
        return pl.ds(pl.multiple_of(c * ch, ch), ch)

    lane = lax.broadcasted_iota(jnp.int32, (ch, LANE), 1)
    lo = lane < half
    sign = jnp.where(lo, -1.0, 1.0)

    def prep(c, _):
        r = sl(c)
        ca, sa = ca_ref[r, :], sa_ref[r, :]
        car, sar = pltpu.roll(ca, half, 1), pltpu.roll(sa, half, 1)
        cos = (jnp.where(lo, ca, car), jnp.where(lo, car, ca))
        sin = (sign * jnp.where(lo, sa, sar), sign * jnp.where(lo, sar, sa))

        def rope(x):
            parts = []
            for p in range(dk // LANE):
                xp = x[:, p * LANE:(p + 1) * LANE]
                parts.append(xp * cos[p] + pltpu.roll(xp, half, 1) * sin[p])
            return jnp.concatenate(parts, axis=-1)

        qs[r, :] = rope(ql_ref[r, :].astype(F32)).astype(BF16)
        ks[r, :] = (rope(kl_ref[r, :].astype(F32)) * kscale).astype(BF16)
        return 0

    lax.fori_loop(0, ncl, prep, 0)
    kcs[...] = (kc_ref[...].astype(F32) * kscale).astype(BF16)

    def log_gamma(dirn):
        d = jnp.full((1, 1), dec_ref[dirn, hd], F32)
        return -(jnp.maximum(-d, 0.0) + jnp.log1p(jnp.exp(-jnp.abs(d))))

    ii = lax.broadcasted_iota(jnp.int32, (ch, ch), 0).astype(F32)
    jj = lax.broadcasted_iota(jnp.int32, (ch, ch), 1).astype(F32)
    pos_k = lax.broadcasted_iota(jnp.int32, (ch, dk), 0).astype(F32)
    lg_f, lg_b = log_gamma(0), log_gamma(1)
    intra_f = jnp.where(ii >= jj, jnp.exp(lg_f * jnp.maximum(ii - jj, 0.0)), 0.0)
    intra_b = jnp.where(jj >= ii, jnp.exp(lg_b * jnp.maximum(jj - ii, 0.0)), 0.0)
    qdec_f = jnp.exp(lg_f * (pos_k + 1.0))
    kdec_f = jnp.exp(lg_f * (ch - 1.0 - pos_k))
    qdec_b = jnp.exp(lg_b * (ch - pos_k))
    kdec_b = jnp.exp(lg_b * pos_k)
    sdec_f = jnp.exp(lg_f * ch)
    sdec_b = jnp.exp(lg_b * ch)

    def step(q_src, k_src, v_src, acc, st, c, intra, qdec, kdec, sdec):
        r = sl(c)
        q, k, v = q_src[r, :], k_src[r, :], v_src[r, :]
        att = _dot_nt(q, k) * intra
        state = st[...]
        out = _dot(att.astype(BF16), v) + _dot((q.astype(F32) * qdec).astype(BF16), state.astype(BF16))
        acc[r, :] = out
        st[...] = state * sdec + _dot_tn((k.astype(F32) * kdec).astype(BF16), v)

    st_f[...] = jnp.zeros_like(st_f)
    st_b[...] = jnp.zeros_like(st_b)

    def scan(q_src, k_src, v_src, acc_f, acc_b, n):
        def body(c, _):
            step(q_src, k_src, v_src, acc_f, st_f, c, intra_f, qdec_f, kdec_f, sdec_f)
            step(q_src, k_src, v_src, acc_b, st_b, n - 1 - c, intra_b, qdec_b, kdec_b, sdec_b)
            return 0
        lax.fori_loop(0, n, body, 0, unroll=int(np.gcd(unroll, n)))

    scan(qc_ref, kcs, vc_ref, af_c, ab_c, ncc)
    scan(qs, ks, vl_ref, af_l, ab_l, ncl)

    def finish(acc_f, acc_b, g_ref, n, row0):
        def body(c, _):
            r = sl(c)
            y = acc_f[r, :] + acc_b[r, :]
            y = y * lax.rsqrt(jnp.mean(y * y, axis=-1, keepdims=True) + EPS)
            g = g_ref[r, :].astype(F32)
            dst = pl.ds(pl.multiple_of(row0 + c * ch, ch), ch)
            o_ref[dst, :] = (y * (g * jax.nn.sigmoid(g))).astype(o_ref.dtype)
            return 0
        lax.fori_loop(0, n, body, 0)

    finish(af_c, ab_c, gc_ref, ncc, 0)
    finish(af_l, ab_l, gl_ref, ncl, c_len)


def _retention(p0, ret_decay, rope_a, rope_b, b, s, c_len, dk, dv, *, unroll=4):
    nh = RET_HEADS
    assert dk == 2 * LANE and dv % LANE == 0 and s % RET_CHUNK == 0 and c_len % RET_CHUNK == 0
    assert s % c_len == 0
    cb0 = (b * s) // c_len
    assert dk == dv
    qo, ko, vo, go = 0, nh, 2 * nh, 3 * nh
    lat = lambda off: pl.BlockSpec((s, dk), lambda i, h, off=off: (i, off + h))
    ctx = lambda off: pl.BlockSpec((c_len, dk), lambda i, h, off=off: (cb0 + i, off + h))
    tab = pl.BlockSpec((s, LANE), lambda i, h: (0, 0))
    return pl.pallas_call(
        functools.partial(_ret_kernel, chunk=RET_CHUNK, kscale=dk ** -0.5, unroll=unroll),
        out_shape=jax.ShapeDtypeStruct((b * (s + c_len), nh * dv), BF16),
        grid=(b, nh),
        in_specs=[pl.BlockSpec(memory_space=pltpu.SMEM),
                  lat(qo), lat(ko), lat(vo), lat(go), ctx(qo), ctx(ko), ctx(vo), ctx(go), tab, tab],
        out_specs=pl.BlockSpec((s + c_len, dv), lambda i, h: (i, h)),
        scratch_shapes=[pltpu.VMEM((s, dk), BF16), pltpu.VMEM((s, dk), BF16), pltpu.VMEM((c_len, dk), BF16),
                        pltpu.VMEM((s, dv), F32), pltpu.VMEM((s, dv), F32),
                        pltpu.VMEM((c_len, dv), F32), pltpu.VMEM((c_len, dv), F32),
                        pltpu.VMEM((dk, dv), F32), pltpu.VMEM((dk, dv), F32)],
        compiler_params=_cparams(("parallel", "arbitrary")),
        name="retention",
    )(ret_decay.astype(F32), p0, p0, p0, p0, p0, p0, p0, p0, rope_a, rope_b)


def _na_kernel(q_ref, k_ref, v_ref, qc_ref, kc_ref, vc_ref, pt_ref, o_ref, *, qrows, krows, rows, w, scale):
    kh, g_rows = NA_KH, LANE // w
    nblk = rows // qrows
    kc, vc = kc_ref[...], vc_ref[...]
    c_len = kc_ref.shape[0]
    neg_tile = jnp.full((w, LANE), NEG_INF, F32)
    lane_row = lax.broadcasted_iota(jnp.int32, (w, LANE), 1) // w
    for blk in range(nblk):
        k0 = int(np.clip(blk * qrows - kh // 2, 0, rows - krows))
        bias_rows = []
        for i in range(qrows):
            qrow = blk * qrows + i
            win0 = int(np.clip(qrow - kh // 2, 0, rows - kh))
            tiles = []
            for jg in range(krows // g_rows):
                kr0 = k0 + jg * g_rows
                ok = [win0 <= kr0 + g < win0 + kh for g in range(g_rows)]
                if not any(ok):
                    tiles.append(neg_tile)
                    continue
                tile = pt_ref[0, kr0 - qrow + kh - 1 + g_rows - 1]
                if not all(ok):
                    lo, hi = ok.index(True), g_rows - ok[::-1].index(True)
                    tile = jnp.where((lane_row >= lo) & (lane_row < hi), tile, NEG_INF)
                tiles.append(tile)
            bias_rows.append(jnp.concatenate(tiles, axis=1))
        bias = jnp.concatenate(bias_rows, axis=0)
        qs = slice(blk * qrows * w, (blk + 1) * qrows * w)
        win = slice(k0 * w, (k0 + krows) * w)
        q = q_ref[qs, :]
        s_loc = _dot_nt(q, k_ref[win, :]) * scale + bias
        s_ctx = _dot_nt(q, kc) * scale
        m = jnp.maximum(jnp.max(s_loc, axis=-1, keepdims=True), jnp.max(s_ctx, axis=-1, keepdims=True))
        p_loc, p_ctx = jnp.exp(s_loc - m), jnp.exp(s_ctx - m)
        l = jnp.sum(p_loc, axis=-1, keepdims=True) + jnp.sum(p_ctx, axis=-1, keepdims=True)
        o = _dot(p_loc.astype(BF16), v_ref[win, :]) + _dot(p_ctx.astype(BF16), vc)
        o_ref[c_len + blk * qrows * w: c_len + (blk + 1) * qrows * w, :] = (o / l).astype(o_ref.dtype)

    sc = _dot_nt(qc_ref[...], kc) * scale
    pc = jnp.exp(sc - jnp.max(sc, axis=-1, keepdims=True))
    oc = _dot(pc.astype(BF16), vc) / jnp.sum(pc, axis=-1, keepdims=True)
    o_ref[0:c_len, :] = oc.astype(o_ref.dtype)


def _na_bias_tiles(rpb):
    w, kh, kw = GRID_W, NA_KH, NA_KW
    g_rows = LANE // w
    qc = np.arange(w)[:, None]
    kc = np.arange(w)[None, :]
    cstart = np.clip(qc - kw // 2, 0, w - kw)
    col_ok = (kc >= cstart) & (kc < cstart + kw)
    dc = np.clip(kc - qc + kw - 1, 0, 2 * kw - 2)
    per_dr = jnp.where(col_ok[None, None], rpb[:, :, dc], NEG_INF)
    n_a = 2 * kh - 1 + g_rows - 1
    idx = np.clip(np.arange(n_a)[:, None] - (g_rows - 1) + np.arange(g_rows)[None, :], 0, 2 * kh - 2)
    tiles = per_dr[:, idx]
    return jnp.transpose(tiles, (0, 1, 3, 2, 4)).reshape(rpb.shape[0], n_a, w, LANE).astype(F32)


def _neighbourhood(p0, rpb, b, s, c_len, hd, col0):
    nh, w, qr = NA_HEADS, GRID_W, NA_QROWS
    rows = s // w
    assert LANE % w == 0
    g_rows = LANE // w
    kr = -(-(qr + NA_KH - 1) // g_rows) * g_rows
    assert rows >= kr and rows % qr == 0 and rows // qr <= 16 and hd == LANE and s % c_len == 0
    cb0 = (b * s) // c_len
    qo, ko, vo = col0, col0 + nh, col0 + 2 * nh
    pt = _na_bias_tiles(rpb)
    lat = lambda off: pl.BlockSpec((s, hd), lambda i, h, off=off: (i, off + h))
    ctx = lambda off: pl.BlockSpec((c_len, hd), lambda i, h, off=off: (cb0 + i, off + h))
    return pl.pallas_call(
        functools.partial(_na_kernel, qrows=qr, krows=kr, rows=rows, w=w, scale=hd ** -0.5),
        out_shape=jax.ShapeDtypeStruct((b * (s + c_len), nh * hd), BF16),
        grid=(b, nh),
        in_specs=[lat(qo), lat(ko), lat(vo), ctx(qo), ctx(ko), ctx(vo),
                  pl.BlockSpec((1,) + pt.shape[1:], lambda i, h: (h, 0, 0, 0))],
        out_specs=pl.BlockSpec((s + c_len, hd), lambda i, h: (i, h)),
        compiler_params=_cparams(("parallel", "parallel")),
        name="neighbourhood",
    )(p0, p0, p0, p0, p0, p0, pt)


def _kpe_kernel(x_ref, t_ref, o_ref, *, n_lat):
    x = x_ref[...].astype(F32)
    half = LANE // 2
    lane = lax.broadcasted_iota(jnp.int32, x.shape, 1)
    r = x * t_ref[...]
    roped = r + pltpu.roll(r, half, 1)
    is_lat = pl.program_id(0) < n_lat
    o_ref[...] = jnp.where(lane < half, jnp.where(is_lat, roped, x), 0.0).astype(o_ref.dtype)


def _mla_kernel(qn_ref, qp_ref, tq_ref, kn_ref, kp_ref, v_ref, knc_ref, kpc_ref, vc_ref, o_ref, k_s, *, scale, tk):
    c_len, s_len = knc_ref.shape[0], kn_ref.shape[0]
    tq = qn_ref.shape[0]
    half = LANE // 2

    @pl.when(pl.program_id(2) == 0)
    def _():
        k_s[0:c_len, 0:LANE] = knc_ref[...]
        k_s[0:c_len, LANE:] = kpc_ref[...]
        k_s[c_len:, 0:LANE] = kn_ref[...]
        k_s[c_len:, LANE:] = kp_ref[...]

    c = scale * np.log2(np.e)
    r = qp_ref[...].astype(F32) * tq_ref[...]
    qpe = r + pltpu.roll(r, half, 1)
    q = jnp.concatenate([(qn_ref[...].astype(F32) * c).astype(BF16), (qpe * c).astype(BF16)], axis=-1)

    def update(k, v, carry):
        m, l, acc = carry
        s = _dot_nt(q, k)
        m_new = jnp.maximum(m, jnp.max(s, axis=-1, keepdims=True))
        alpha = jnp.exp2(m - m_new)
        p = jnp.exp2(s - jnp.tile(m_new, (1, s.shape[1] // LANE)))
        l = alpha * l + jnp.sum(p, axis=-1, keepdims=True)
        acc = alpha * acc + _dot(p.astype(BF16), v)
        return m_new, l, acc

    carry = (jnp.full((tq, LANE), NEG_INF, F32), jnp.zeros((tq, LANE), F32), jnp.zeros((tq, LANE), F32))
    carry = update(k_s[0:c_len, :], vc_ref[...], carry)
    for j in range(s_len // tk):
        carry = update(k_s[c_len + j * tk: c_len + (j + 1) * tk, :], v_ref[j * tk:(j + 1) * tk, :], carry)
    _, l, acc = carry
    o_ref[...] = (acc / l).astype(o_ref.dtype)


def _mla_attention(q, kv, kpe, tab, b, s, c_len, *, tq=512, tk=2048):
    nh = MLA_HEADS
    assert MLA_NOPE == LANE and MLA_V == LANE and 2 * MLA_ROPE == LANE and c_len % LANE == 0
    tq, tk = _tile(s, tq, SUBLANE), _tile(s, tk)
    nq = s // tq
    cb0 = (b * s) // c_len
    return pl.pallas_call(
        functools.partial(_mla_kernel, scale=(MLA_NOPE + MLA_ROPE) ** -0.5, tk=tk),
        out_shape=jax.ShapeDtypeStruct((b * s, nh * MLA_V), BF16),
        grid=(b, nh, nq),
        in_specs=[pl.BlockSpec((tq, LANE), lambda i, h, a: (i * nq + a, h)),
                  pl.BlockSpec((tq, LANE), lambda i, h, a: (i * nq + a, nh + h)),
                  pl.BlockSpec((tq, LANE), lambda i, h, a: (a, 0)),
                  pl.BlockSpec((s, LANE), lambda i, h, a: (i, h)),
                  pl.BlockSpec((s, LANE), lambda i, h, a: (i, 0)),
                  pl.BlockSpec((s, LANE), lambda i, h, a: (i, nh + h)),
                  pl.BlockSpec((c_len, LANE), lambda i, h, a: (cb0 + i, h)),
                  pl.BlockSpec((c_len, LANE), lambda i, h, a: (cb0 + i, 0)),
                  pl.BlockSpec((c_len, LANE), lambda i, h, a: (cb0 + i, nh + h))],
        out_specs=pl.BlockSpec((tq, MLA_V), lambda i, h, a: (i * nq + a, h)),
        scratch_shapes=[pltpu.VMEM((c_len + s, 2 * LANE), BF16)],
        compiler_params=_cparams(("parallel", "parallel", "arbitrary")),
        name="mla_attention",
    )(q, q, tab, kv, kpe, kv, kv, kpe, kv)


def _lru_kernel(xl_ref, xc_ref, yl_ref, cw_ref, cb_ref, wa_ref, wi_ref, ba_ref, bi_ref, lam_ref, o_ref,
                xp_l, xp_c, a_f, u_f, a_b, u_b, *, rows_per_step):
    s_len, c_len = xl_ref.shape[0], xc_ref.shape[0]
    pad = SUBLANE
    left = (LRU_CONV - 1) // 2
    cw, cb = cw_ref[...], cb_ref[...]

    def softplus(x):
        return jnp.maximum(x, 0.0) + jnp.log1p(jnp.exp(-jnp.abs(x)))

    sp = [softplus(-lam_ref[d:d + 1, :]) for d in range(2)]

    def gates(xp, n, f_off, b_off):
        step = min(rows_per_step, n)
        for r0 in range(0, n, step):
            x = cb
            for j in range(LRU_CONV):
                x = x + cw[j:j + 1, :] * xp[pad - left + j + r0: pad - left + j + r0 + step, :]
            xb = x.astype(BF16)

            def blockdiag(w_ref, d):
                return jnp.concatenate([_dot(xb[:, g * LANE:(g + 1) * LANE], w_ref[d, g])
                                        for g in range(w_ref.shape[1])], axis=-1)

            for d, (a_ref, u_ref, off) in enumerate(((a_f, u_f, f_off), (a_b, u_b, b_off))):
                rg = _sigmoid(blockdiag(wa_ref, d) + ba_ref[d:d + 1, :])
                ig = _sigmoid(blockdiag(wi_ref, d) + bi_ref[d:d + 1, :])
                log_a = -LRU_C * rg * sp[d]
                a_ref[off + r0: off + r0 + step, :] = jnp.exp(log_a)
                th = jnp.tanh(log_a)
                u_ref[off + r0: off + r0 + step, :] = jnp.sqrt(-2.0 * th / (1.0 - th)) * ig * x

    zeros = jnp.zeros((pad, xl_ref.shape[1]), F32)
    for xp, src, n in ((xp_l, xl_ref, s_len), (xp_c, xc_ref, c_len)):
        xp[0:pad, :] = zeros
        xp[pad + n: 2 * pad + n, :] = zeros
        xp[pad: pad + n, :] = src[...].astype(F32)
    gates(xp_c, c_len, 0, s_len)
    gates(xp_l, s_len, c_len, 0)

    n_tiles = (s_len + c_len) // SUBLANE
    row = lax.broadcasted_iota(jnp.int32, (SUBLANE, xl_ref.shape[1]), 0)

    def body(i, carry):
        hf, hb = carry
        rf = pl.ds(pl.multiple_of(i * SUBLANE, SUBLANE), SUBLANE)
        a, u = a_f[rf, :], u_f[rf, :]
        for sft in (1, 2, 4):
            m = row >= sft
            u = jnp.where(m, a * pltpu.roll(u, sft, 0) + u, u)
            a = jnp.where(m, a * pltpu.roll(a, sft, 0), a)
        h = a * hf + u
        u_f[rf, :] = h
        hf = h[SUBLANE - 1:SUBLANE, :]
        rb = pl.ds(pl.multiple_of((n_tiles - 1 - i) * SUBLANE, SUBLANE), SUBLANE)
        a, u = a_b[rb, :], u_b[rb, :]
        for sft in (1, 2, 4):
            m = row < SUBLANE - sft
            u = jnp.where(m, a * pltpu.roll(u, SUBLANE - sft, 0) + u, u)
            a = jnp.where(m, a * pltpu.roll(a, SUBLANE - sft, 0), a)
        h = a * hb + u
        u_b[rb, :] = h
        hb = h[0:1, :]
        return hf, hb

    h0 = jnp.zeros((1, xl_ref.shape[1]), F32)
    lax.fori_loop(0, n_tiles, body, (h0, h0), unroll=4)

    step = min(rows_per_step, s_len)
    for r0 in range(0, s_len, step):
        y = yl_ref[r0:r0 + step, :].astype(F32)
        h = u_f[c_len + r0: c_len + r0 + step, :] + u_b[r0:r0 + step, :]
        o_ref[r0:r0 + step, :] = (h * jax.nn.gelu(y)).astype(o_ref.dtype)


def _rglru(p1, conv_w, conv_b, wa, wi, ba, bi, lam, b, s, c_len, x_col0, y_col0, *, group=2):
    nblk, bd = wa.shape[1], wa.shape[2]
    width = nblk * bd
    group = int(np.gcd(np.gcd(group, nblk), np.gcd(x_col0, y_col0)))
    gw = group * bd
    assert bd == LANE and s % c_len == 0 and (s + c_len) % (4 * SUBLANE) == 0
    cb0 = (b * s) // c_len
    t = s + c_len
    xc, yc = x_col0 // group, y_col0 // group
    vec2 = pl.BlockSpec((2, gw), lambda i, k: (0, k))
    wspec = pl.BlockSpec((2, group, bd, bd), lambda i, k: (0, k, 0, 0))
    return pl.pallas_call(
        functools.partial(_lru_kernel, rows_per_step=512),
        out_shape=jax.ShapeDtypeStruct((b * s, width), BF16),
        grid=(b, nblk // group),
        in_specs=[pl.BlockSpec((s, gw), lambda i, k: (i, xc + k)),
                  pl.BlockSpec((c_len, gw), lambda i, k: (cb0 + i, xc + k)),
                  pl.BlockSpec((s, gw), lambda i, k: (i, yc + k)),
                  pl.BlockSpec((LRU_CONV, gw), lambda i, k: (0, k)),
                  pl.BlockSpec((1, gw), lambda i, k: (0, k)),
                  wspec, wspec, vec2, vec2, vec2],
        out_specs=pl.BlockSpec((s, gw), lambda i, k: (i, k)),
        scratch_shapes=[pltpu.VMEM((s + 2 * SUBLANE, gw), F32), pltpu.VMEM((c_len + 2 * SUBLANE, gw), F32),
                        pltpu.VMEM((t, gw), F32), pltpu.VMEM((t, gw), F32),
                        pltpu.VMEM((t, gw), F32), pltpu.VMEM((t, gw), F32)],
        compiler_params=_cparams(("parallel", "parallel")),
        name="rglru",
    )(p1, p1, p1, conv_w.astype(F32), conv_b.reshape(1, width).astype(F32),
      wa.astype(BF16), wi.astype(BF16), ba.astype(F32), bi.astype(F32), lam.astype(F32))


def _rope_angles(n, rot_dim):
    t = np.arange(n)
    row = (t // GRID_W).astype(np.float32)
    col = (t % GRID_W).astype(np.float32)
    nf = rot_dim // 4
    inv = jnp.asarray(ROPE_BASE, F32) ** (-jnp.arange(nf, dtype=F32) / nf)
    return row[:, None] * inv, col[:, None] * inv


def _swap_perm(rot_dim):
    q = rot_dim // 4
    idx = np.arange(rot_dim).reshape(4, q)
    return idx[[1, 0, 3, 2]].reshape(-1)


def kernel(x, c, ctx, c_ctx, w_mod, b_mod, norm_g, w_in_even, w_out_even, ret_decay, na_rpb, ffn_w_gate, ffn_w_up,
           ffn_w_down, w_in_odd, mla_q_norm, mla_w_uq, mla_kv_norm, mla_w_ukv, lru_conv_w, lru_conv_b, lru_w_a,
           lru_b_a, lru_w_i, lru_b_i, lru_lambda, w_out_odd, router_w, moe_w_gate, moe_w_up, moe_w_down):
    b, s, d = x.shape
    c_len = ctx.shape[1]
    n_lat, n_ctx = b * s, b * c_len
    n_tok = n_lat + n_ctx
    depth = w_mod.shape[0]
    assert depth == 2 and w_in_even.shape[0] == 1 and w_in_odd.shape[0] == 1
    assert n_ctx % SUBLANE == 0 and s % c_len == 0

    mod_rows = -(-(b + 1) // SUBLANE) * SUBLANE
    s_in = jnp.concatenate([c, c_ctx[None], jnp.zeros((mod_rows - b - 1, d), F32)], axis=0)
    mods = _modulation(s_in, w_mod, b_mod).reshape(depth, mod_rows, 6, 1, d)
    mod = lambda l, j: mods[l, :, j]
    mod_row = lambda r0: jnp.where(r0 < n_lat, r0 // s, b)


    g = norm_g[0]
    (h,) = _resnorm(x, n_tok, mod_row, x_ctx=ctx, gn=g[0], sc=mod(0, 0), sh=mod(0, 1))
    w_in = w_in_even[0].astype(BF16)
    p0 = _mm([(h, d, 0)], [(w_in, d, 0)], n_tok, w_in.shape[1], BF16, tn=1024, name="in_proj_even")
    ret_w = w_out_even.shape[1] - NA_HEADS * LANE
    dk = ret_w // RET_HEADS
    ar, ac = _rope_angles(s, dk)
    rope_a = jnp.concatenate([jnp.cos(ar), jnp.cos(ac)], axis=-1)
    rope_b = jnp.concatenate([jnp.sin(ar), jnp.sin(ac)], axis=-1)
    mix_a = _retention(p0, ret_decay[0], rope_a, rope_b, b, s, c_len, dk, dk)
    mix_b = _neighbourhood(p0, na_rpb[0], b, s, c_len, LANE, 4 * ret_w // LANE)
    w_out = w_out_even[0].astype(BF16)
    half_k = w_out.shape[0] // 2
    assert mix_a.shape[1] == half_k and mix_b.shape[1] == half_k
    o = _mm([(mix_a, half_k, 0), (mix_b, half_k, 0)], [(w_out, half_k, 0), (w_out, half_k, 1)],
            n_tok, d, BF16, name="out_proj_even")
    xs, h = _resnorm(x, n_tok, mod_row, x_ctx=ctx, y=o, y_batch_major=True, gate=mod(0, 2), gy=g[1], gn=g[2],
                     sc=mod(0, 3), sh=mod(0, 4))
    y = _ffn(h, ffn_w_gate[0].astype(BF16), ffn_w_up[0].astype(BF16), ffn_w_down[0].astype(BF16))
    g1 = norm_g[1]
    xs, h = _resnorm(xs, n_tok, mod_row, y=y, gate=mod(0, 5), gy=g[3], gn=g1[0], sc=mod(1, 0), sh=mod(1, 1))

    g = g1
    qr, kvr, rr = MLA_Q_RANK, MLA_KV_RANK, MLA_ROPE
    lw = lru_w_a.shape[2] * lru_w_a.shape[3]
    perm = _swap_perm(rr)
    wi = w_in_odd[0]
    o_q, o_kv, o_kr, o_lx, o_ly = 0, qr, qr + kvr, qr + kvr + rr, qr + kvr + rr + lw
    kr_w = wi[:, o_kr:o_kr + rr]
    w_in = jnp.concatenate([wi[:, o_q:o_q + qr], wi[:, o_kv:o_kv + kvr], wi[:, o_lx:o_lx + lw],
                            wi[:, o_ly:o_ly + lw], kr_w, kr_w[:, perm]], axis=-1).astype(BF16)
    assert qr % LANE == 0 and kvr % LANE == 0 and lw % LANE == 0 and 2 * rr == LANE
    p1 = _mm([(h, d, 0)], [(w_in, d, 0)], n_tok, w_in.shape[1], BF16, tn=896, name="in_proj_odd")
    nh = MLA_HEADS
    wq = mla_w_uq[0].reshape(qr, nh, MLA_NOPE + rr)
    wq_pe = wq[:, :, MLA_NOPE:]
    wq = jnp.concatenate([wq[:, :, :MLA_NOPE].reshape(qr, nh * MLA_NOPE),
                          jnp.concatenate([wq_pe, wq_pe[:, :, perm]], axis=-1).reshape(qr, nh * LANE)],
                         axis=-1).astype(BF16)
    wkv = mla_w_ukv[0].reshape(kvr, nh, MLA_NOPE + MLA_V)
    wkv = jnp.concatenate([wkv[:, :, :MLA_NOPE].reshape(kvr, nh * MLA_NOPE),
                           wkv[:, :, MLA_NOPE:].reshape(kvr, nh * MLA_V)], axis=-1).astype(BF16)
    assert qr % kvr == 0
    q = _mm([(p1, qr, 0)], [(wq, qr, 0)], n_lat, wq.shape[1], BF16, norm_g=mla_q_norm[0], name="mla_q")
    kv = _mm([(p1, kvr, qr // kvr)], [(wkv, kvr, 0)], n_tok, wkv.shape[1], BF16, norm_g=mla_kv_norm[0],
             name="mla_kv")
    ar, ac = _rope_angles(s, rr)
    cos4 = jnp.concatenate([jnp.cos(ar), jnp.cos(ar), jnp.cos(ac), jnp.cos(ac)], axis=-1)
    sin4 = jnp.concatenate([-jnp.sin(ar), jnp.sin(ar), -jnp.sin(ac), jnp.sin(ac)], axis=-1)
    tab = jnp.concatenate([cos4, sin4], axis=-1)
    kr_cb = (qr + kvr + 2 * lw) // LANE
    tr = _tile(s, 512, SUBLANE)
    assert n_ctx % tr == 0 or tr % n_ctx == 0
    tr = min(tr, n_ctx) if n_ctx % tr else tr
    nt_lat, nt_seq = n_lat // tr, s // tr
    kpe = pl.pallas_call(
        functools.partial(_kpe_kernel, n_lat=nt_lat),
        out_shape=jax.ShapeDtypeStruct((n_tok, LANE), BF16),
        grid=(n_tok // tr,),
        in_specs=[pl.BlockSpec((tr, LANE), lambda m: (m, kr_cb)),
                  pl.BlockSpec((tr, LANE), lambda m: (jnp.where(m < nt_lat, m % nt_seq, 0), 0))],
        out_specs=pl.BlockSpec((tr, LANE), lambda m: (m, 0)),
        compiler_params=_cparams(("parallel",)),
        name="mla_kpe",
    )(p1, tab)
    att = _mla_attention(q, kv, kpe, tab, b, s, c_len)
    lru = _rglru(p1, lru_conv_w[0], lru_conv_b[0], lru_w_a[0], lru_w_i[0], lru_b_a[0], lru_b_i[0], lru_lambda[0],
                 b, s, c_len, (qr + kvr) // LANE, (qr + kvr + lw) // LANE)
    w_out = w_out_odd[0].astype(BF16)
    half_k = w_out.shape[0] // 2
    assert att.shape[1] == half_k and lru.shape[1] == half_k
    o = _mm([(att, half_k, 0), (lru, half_k, 0)], [(w_out, half_k, 0), (w_out, half_k, 1)],
            n_lat, d, BF16, name="out_proj_odd")
    xl, hpk, route = _resnorm(xs, n_lat, mod_row, y=o, gate=mod(1, 2), gy=g[1], gn=g[2], sc=mod(1, 3),
                              sh=mod(1, 4), router_w=router_w[0])
    ypk = _moe(hpk, route, moe_w_gate[0].astype(BF16), moe_w_up[0].astype(BF16), moe_w_down[0].astype(BF16))
    xl = _moe_finish(xl, ypk, route, mod_row, mod(1, 5), g[3])
    return xl.reshape(b, s, d)
```

```python
import functools

import numpy as np
import jax
import jax.numpy as jnp
from jax import lax
from jax.experimental import pallas as pl
from jax.experimental.pallas import tpu as pltpu

F32 = jnp.float32
BF16 = jnp.bfloat16

EPS = 1e-6
NEG_INF = -1e30
ROPE_BASE = 10000.0
GRID_W = 64
RET_HEADS = 8
RET_CHUNK = 128
NA_HEADS = 16
NA_KH = 8
NA_KW = 16
NA_QROWS = 8
MLA_HEADS = 16
MLA_Q_RANK = 1536
MLA_KV_RANK = 512
MLA_NOPE = 128
MLA_ROPE = 64
MLA_V = 128
LRU_C = 8.0
LRU_CONV = 4
TOP_K = 2

LANE = 128
SUBLANE = 8
VMEM_LIMIT = 52 * 1024 * 1024


def _cparams(sem):
    return pltpu.CompilerParams(dimension_semantics=sem, vmem_limit_bytes=VMEM_LIMIT)


def _tile(n, target, mult=LANE):
    if n <= target:
        return n
    best = None
    for t in range(mult, target + 1, mult):
        if n % t == 0:
            best = t
    assert best is not None, (n, target, mult)
    return best


def _dot(a, b):
    return jnp.dot(a, b, preferred_element_type=F32)


def _dot_nt(a, b):
    return lax.dot_general(a, b, (((1,), (1,)), ((), ())), preferred_element_type=F32)


def _dot_tn(a, b):
    return lax.dot_general(a, b, (((0,), (0,)), ((), ())), preferred_element_type=F32)


def _sigmoid(x):
    return 0.5 * jnp.tanh(0.5 * x) + 0.5


def _rms(x, g):
    return x * lax.rsqrt(jnp.mean(x * x, axis=-1, keepdims=True) + EPS) * g


def _mod_kernel(s_ref, w_ref, b_ref, o_ref):
    s = s_ref[...]
    s = (s * jax.nn.sigmoid(s)).astype(BF16)
    o_ref[0] = _dot(s, w_ref[0].astype(BF16)) + b_ref[0]


def _modulation(s_in, w_mod, b_mod):
    nl, d, n = w_mod.shape
    rows = s_in.shape[0]
    tn = _tile(n, 512)
    return pl.pallas_call(
        _mod_kernel,
        out_shape=jax.ShapeDtypeStruct((nl, rows, n), F32),
        grid=(nl, n // tn),
        in_specs=[pl.BlockSpec((rows, d), lambda l, j: (0, 0)),
                  pl.BlockSpec((1, d, tn), lambda l, j: (l, 0, j)),
                  pl.BlockSpec((1, 1, tn), lambda l, j: (l, 0, j))],
        out_specs=pl.BlockSpec((1, rows, tn), lambda l, j: (l, 0, j)),
        compiler_params=_cparams(("arbitrary", "arbitrary")),
        name="modulation",
    )(s_in, w_mod, b_mod.reshape(nl, 1, n))


def _pack_halves(v):
    half = v.shape[1] // 2
    lo = lax.bitcast_convert_type(v[:, :half].astype(BF16).astype(F32), jnp.uint32) >> 16
    hi = lax.bitcast_convert_type(v[:, half:].astype(BF16).astype(F32), jnp.uint32) & jnp.uint32(0xFFFF0000)
    return lo | hi


def _unpack_halves(u):
    lo = lax.bitcast_convert_type(u << 16, F32)
    hi = lax.bitcast_convert_type(u & jnp.uint32(0xFFFF0000), F32)
    return jnp.concatenate([lo, hi], axis=-1)


def _route_top2(logits, n_exp):
    lane = lax.broadcasted_iota(jnp.int32, logits.shape, 1)
    big = jnp.int32(2 ** 30)
    logits = jnp.where(lane < n_exp, logits, -jnp.inf)
    m1 = jnp.max(logits, axis=-1, keepdims=True)
    i1 = jnp.min(jnp.where(logits == m1, lane, big), axis=-1, keepdims=True)
    rest = jnp.where(lane == i1, -jnp.inf, logits)
    m2 = jnp.max(rest, axis=-1, keepdims=True)
    i2 = jnp.min(jnp.where(rest == m2, lane, big), axis=-1, keepdims=True)
    e2 = jnp.exp(m2 - m1)
    g1 = 1.0 / (1.0 + e2)
    g2 = e2 / (1.0 + e2)
    out = jnp.where(lane == 0, i1.astype(F32), 0.0) + jnp.where(lane == 1, i2.astype(F32), 0.0)
    return out + jnp.where(lane == 2, g1, 0.0) + jnp.where(lane == 3, g2, 0.0)


def _resnorm_kernel(*refs, has_y, has_next, n_exp, lat_tiles):
    it = iter(refs)
    x_ref = next(it)
    xc_ref = next(it) if lat_tiles else None
    if has_y:
        y_ref, gate_ref, gy_ref = next(it), next(it), next(it)
    if has_next:
        gn_ref, sc_ref, sh_ref = next(it), next(it), next(it)
    if n_exp:
        rw_ref = next(it)
    if lat_tiles:
        x = jnp.where(pl.program_id(0) < lat_tiles, x_ref[0], xc_ref[0])
    else:
        x = x_ref[...]
    if has_y:
        xo_ref = next(it)
        x = x + gate_ref[0] * _rms(y_ref[...].astype(F32), gy_ref[...])
        xo_ref[...] = x
    if has_next:
        h_ref = next(it)
        h = _rms(x, gn_ref[...]) * (1.0 + sc_ref[0]) + sh_ref[0]
        if n_exp:
            h_ref[...] = _pack_halves(h)
            route_ref = next(it)
            logits = jnp.dot(h, rw_ref[...], preferred_element_type=F32, precision=lax.Precision.HIGHEST)
            route_ref[...] = _route_top2(logits, n_exp)
        else:
            h_ref[...] = h.astype(BF16)


def _resnorm(x, nrows, mod_row, *, x_ctx=None, y=None, y_batch_major=False, gate=None, gy=None, gn=None, sc=None,
             sh=None, router_w=None, tr=256):
    d = x.shape[-1]
    tr = _tile(nrows, tr, SUBLANE)
    has_y, has_next = y is not None, gn is not None
    n_exp = 0 if router_w is None else router_w.shape[1]
    row = pl.BlockSpec((tr, d), lambda m: (m, 0))
    vec = pl.BlockSpec((1, d), lambda m: (0, 0))
    modv = pl.BlockSpec((1, 1, d), lambda m: (mod_row(m * tr), 0, 0))
    args, specs, outs, ospecs = [x], [row], [], []
    lat_tiles = 0
    if x_ctx is not None:
        (_, s_len, _), c_len = x.shape, x_ctx.shape[1]
        tr = _tile(int(np.gcd(s_len, c_len)), tr, SUBLANE)
        spt, cpt = s_len // tr, c_len // tr
        lat_tiles = x.shape[0] * spt
        lat = lambda m: jnp.minimum(m, lat_tiles - 1)
        cx = lambda m: jnp.maximum(m - lat_tiles, 0)
        args = [x, x_ctx]
        specs = [pl.BlockSpec((1, tr, d), lambda m: (lat(m) // spt, lat(m) % spt, 0)),
                 pl.BlockSpec((1, tr, d), lambda m: (cx(m) // cpt, cx(m) % cpt, 0))]
        row = pl.BlockSpec((tr, d), lambda m: (m, 0))
        modv = pl.BlockSpec((1, 1, d), lambda m: (mod_row(m * tr), 0, 0))
    yrow = row
    if y_batch_major:
        per = spt + cpt
        yrow = pl.BlockSpec((tr, d), lambda m: (jnp.where(m < lat_tiles, (m // spt) * per + cpt + m % spt,
                                                          (cx(m) // cpt) * per + cx(m) % cpt), 0))
    if has_y:
        args += [y, gate, gy.reshape(1, d)]
        specs += [yrow, modv, vec]
        outs.append(jax.ShapeDtypeStruct((nrows, d), F32))
        ospecs.append(row)
    if has_next:
        args += [gn.reshape(1, d), sc, sh]
        specs += [vec, modv, modv]
        if n_exp:
            assert n_exp <= LANE and TOP_K == 2
            args.append(jnp.pad(router_w.astype(F32), ((0, 0), (0, LANE - n_exp))))
            specs.append(pl.BlockSpec((d, LANE), lambda m: (0, 0)))
            outs += [jax.ShapeDtypeStruct((nrows, d // 2), jnp.uint32), jax.ShapeDtypeStruct((nrows, LANE), F32)]
            ospecs += [pl.BlockSpec((tr, d // 2), lambda m: (m, 0)), pl.BlockSpec((tr, LANE), lambda m: (m, 0))]
        else:
            outs.append(jax.ShapeDtypeStruct((nrows, d), BF16))
            ospecs.append(row)
    res = pl.pallas_call(
        functools.partial(_resnorm_kernel, has_y=has_y, has_next=has_next, n_exp=n_exp, lat_tiles=lat_tiles),
        out_shape=outs, grid=(nrows // tr,), in_specs=specs, out_specs=ospecs,
        compiler_params=_cparams(("parallel",)),
        name="resnorm",
    )(*args)
    return res


def _mm_kernel(*refs, nparts, has_norm):
    x_refs, w_refs = refs[:nparts], refs[nparts:2 * nparts]
    pos = 2 * nparts
    if has_norm:
        g_ref, o_ref, h_ref = refs[pos], refs[pos + 1], refs[pos + 2]

        @pl.when(pl.program_id(1) == 0)
        def _():
            h_ref[...] = _rms(x_refs[0][...].astype(F32), g_ref[...]).astype(BF16)

        acc = _dot(h_ref[...], w_refs[0][...])
    else:
        o_ref = refs[pos]
        acc = _dot(x_refs[0][...], w_refs[0][...])
        for i in range(1, nparts):
            acc = acc + _dot(x_refs[i][...], w_refs[i][...])
    o_ref[...] = acc.astype(o_ref.dtype)


def _mm(xs, ws, m, n, out_dtype, *, norm_g=None, tm=1024, tn=512, name="mm"):
    tm, tn = _tile(m, tm, SUBLANE), _tile(n, tn)
    has_norm = norm_g is not None
    args, specs = [], []
    for (a, k, cb) in xs:
        args.append(a)
        specs.append(pl.BlockSpec((tm, k), lambda i, j, cb=cb: (i, cb)))
    for (w, k, rb) in ws:
        args.append(w)
        specs.append(pl.BlockSpec((k, tn), lambda i, j, rb=rb: (rb, j)))
    scratch = []
    if has_norm:
        k0 = xs[0][1]
        args.append(norm_g.reshape(1, k0).astype(F32))
        specs.append(pl.BlockSpec((1, k0), lambda i, j: (0, 0)))
        scratch.append(pltpu.VMEM((tm, k0), BF16))
    return pl.pallas_call(
        functools.partial(_mm_kernel, nparts=len(xs), has_norm=has_norm),
        out_shape=jax.ShapeDtypeStruct((m, n), out_dtype),
        grid=(m // tm, n // tn), in_specs=specs,
        out_specs=pl.BlockSpec((tm, tn), lambda i, j: (i, j)),
        scratch_shapes=scratch,
        compiler_params=_cparams(("parallel", "arbitrary")),
        name=name,
    )(*args)


def _swiglu_step(h, wg, wu, wd):
    g = _dot(h, wg)
    u = _dot(h, wu)
    return _dot((g * jax.nn.sigmoid(g) * u).astype(BF16), wd)


def _ffn_kernel(h_ref, wg_ref, wu_ref, wd_ref, o_ref):
    @pl.when(pl.program_id(1) == 0)
    def _():
        o_ref[...] = jnp.zeros_like(o_ref)

    o_ref[...] += _swiglu_step(h_ref[...], wg_ref[...], wu_ref[...], wd_ref[...])


def _ffn(h, wg, wu, wd, *, tm=512, tf=256, single=0):
    m, d = h.shape
    f = wg.shape[1]
    tm, tf = _tile(m, tm, SUBLANE), _tile(f, tf)
    once = dict(pipeline_mode=pl.Buffered(1)) if single else {}
    return pl.pallas_call(
        _ffn_kernel,
        out_shape=jax.ShapeDtypeStruct((m, d), F32),
        grid=(m // tm, f // tf),
        in_specs=[pl.BlockSpec((tm, d), lambda i, j: (i, 0), **once),
                  pl.BlockSpec((d, tf), lambda i, j: (0, j)),
                  pl.BlockSpec((d, tf), lambda i, j: (0, j)),
                  pl.BlockSpec((tf, d), lambda i, j: (j, 0))],
        out_specs=pl.BlockSpec((tm, d), lambda i, j: (i, 0), **once),
        compiler_params=_cparams(("parallel", "arbitrary")),
        name="ffn",
    )(h, wg, wu, wd)


def _moe_kernel(te_ref, nu_ref, src_ref, aid_ref, h_hbm, wg_ref, wu_ref, wd_ref, y_hbm,
                hbuf, hb, acc, stage, gsem, ssem, *, nf, p_rows):
    i, f = pl.program_id(0), pl.program_id(1)
    nt = pl.num_programs(0)
    tm = hb.shape[0]
    rows = tm // nf
    n_used = nu_ref[0]
    used = i < n_used
    slot = i % 2
    r0 = f * rows

    def gather_row(tile, sl, r):
        tok = src_ref[tile * tm + r]
        pltpu.make_async_copy(h_hbm.at[pl.ds(tok, 1)], hbuf.at[sl, pl.ds(r, 1)], gsem.at[sl]).start()

    def scatter_row(sl, r, dst):
        pltpu.make_async_copy(stage.at[sl, pl.ds(r, 1)], y_hbm.at[pl.ds(dst, 1)], ssem.at[sl]).start()

    def tile_done(buf, sem):
        pltpu.make_async_copy(buf, buf, sem).wait()

    def scatter_slice():
        prev = jnp.maximum(i - 1, 0)
        for r in range(rows):
            dst = jnp.where(i >= 1, aid_ref[prev * tm + r0 + r], p_rows + r0 + r)
            scatter_row(1 - slot, r0 + r, dst)

    @pl.when(jnp.logical_and(f == 0, i == 0))
    def _():
        stage[1] = jnp.zeros(stage.shape[1:], stage.dtype)

        def body(r, _):
            gather_row(0, 0, r)
            return 0
        lax.fori_loop(0, tm, body, 0, unroll=8)

    @pl.when(jnp.logical_and(f == 0, i <= n_used))
    def _():
        tile_done(hbuf.at[slot], gsem.at[slot])

    @pl.when(jnp.logical_and(f == 0, used))
    def _():
        hb[...] = _unpack_halves(hbuf[slot]).astype(BF16)
        acc[...] = jnp.zeros_like(acc)

    @pl.when(used)
    def _():
        for r in range(rows):
            gather_row(i + 1, 1 - slot, r0 + r)
        scatter_slice()
        acc[...] += _swiglu_step(hb[...], wg_ref[0], wu_ref[0], wd_ref[0])

    @pl.when(jnp.logical_not(used))
    def _():
        scatter_slice()

    @pl.when(f == nf - 1)
    def _():
        @pl.when(i >= 1)
        def _():
            tile_done(stage.at[slot], ssem.at[slot])

        @pl.when(used)
        def _():
            stage[slot] = _pack_halves(acc[...])

        @pl.when(jnp.logical_not(used))
        def _():
            stage[slot] = jnp.zeros(stage.shape[1:], stage.dtype)

        @pl.when(i == nt - 1)
        def _():
            def body(r, _):
                scatter_row(slot, r, aid_ref[i * tm + r])
                return 0
            lax.fori_loop(0, tm, body, 0, unroll=8)
            tile_done(stage.at[slot], ssem.at[slot])
            tile_done(stage.at[1 - slot], ssem.at[1 - slot])


def _moe_route(route, n_exp, tm):
    n = route.shape[0]
    e = route[:, :TOP_K].astype(jnp.int32).T.reshape(-1)
    onehot = (e[:, None] == jnp.arange(n_exp, dtype=jnp.int32)[None, :]).astype(jnp.int32)
    rank = jnp.sum((jnp.cumsum(onehot, axis=0) - onehot) * onehot, axis=1)
    counts = jnp.sum(onehot, axis=0)
    padded = ((counts + tm - 1) // tm) * tm
    ends = jnp.cumsum(padded)
    starts = ends - padded
    dest = starts[e] + rank
    p_rows = TOP_K * n + n_exp * tm
    n_tiles = p_rows // tm
    aid = jnp.full((p_rows,), -1, jnp.int32).at[dest].set(jnp.arange(TOP_K * n, dtype=jnp.int32))
    is_pad = aid < 0
    src = jnp.where(is_pad, 0, aid % n)
    aid = jnp.where(is_pad, TOP_K * n + jnp.cumsum(is_pad.astype(jnp.int32)) - 1, aid)
    n_used = (ends[-1] // tm).astype(jnp.int32)
    tile0 = jnp.arange(n_tiles, dtype=jnp.int32) * tm
    te = jnp.minimum(jnp.sum((tile0[:, None] >= ends[None, :]).astype(jnp.int32), axis=1), n_exp - 1)
    te = jnp.where(jnp.arange(n_tiles) < n_used, te, te[jnp.maximum(n_used - 1, 0)])
    return te, n_used.reshape(1), src, aid


def _moe(hpk, route, wg, wu, wd, *, tm=512, tf=256):
    n, dh = hpk.shape
    d = 2 * dh
    n_exp, _, f = wg.shape
    tm, tf = _tile(n, tm, SUBLANE), _tile(f, tf)
    te, n_used, src, aid = _moe_route(route, n_exp, tm)
    p_rows = src.shape[0]
    nf = f // tf
    assert tm % nf == 0
    frozen = lambda i, j, nu: jnp.where(i < nu[0], j, nf - 1)
    grid_spec = pltpu.PrefetchScalarGridSpec(
        num_scalar_prefetch=4,
        grid=(p_rows // tm, nf),
        in_specs=[pl.BlockSpec(memory_space=pl.ANY),
                  pl.BlockSpec((1, d, tf), lambda i, j, te, nu, s, a: (te[i], 0, frozen(i, j, nu))),
                  pl.BlockSpec((1, d, tf), lambda i, j, te, nu, s, a: (te[i], 0, frozen(i, j, nu))),
                  pl.BlockSpec((1, tf, d), lambda i, j, te, nu, s, a: (te[i], frozen(i, j, nu), 0))],
        out_specs=pl.BlockSpec(memory_space=pl.ANY),
        scratch_shapes=[pltpu.VMEM((2, tm, dh), jnp.uint32), pltpu.VMEM((tm, d), BF16), pltpu.VMEM((tm, d), F32),
                        pltpu.VMEM((2, tm, dh), jnp.uint32), pltpu.SemaphoreType.DMA((2,)),
                        pltpu.SemaphoreType.DMA((2,))],
    )
    return pl.pallas_call(
        functools.partial(_moe_kernel, nf=nf, p_rows=p_rows),
        out_shape=jax.ShapeDtypeStruct((p_rows + tm, dh), jnp.uint32),
        grid_spec=grid_spec,
        compiler_params=_cparams(("arbitrary", "arbitrary")),
        name="moe",
    )(te, n_used, src, aid, hpk, wg, wu, wd)


def _moe_finish_kernel(x_ref, y1_ref, y2_ref, r_ref, gate_ref, gy_ref, o_ref):
    r = r_ref[...]
    lane = lax.broadcasted_iota(jnp.int32, r.shape, 1)
    g1 = jnp.sum(jnp.where(lane == 2, r, 0.0), axis=-1, keepdims=True)
    g2 = jnp.sum(jnp.where(lane == 3, r, 0.0), axis=-1, keepdims=True)
    y = g1 * _unpack_halves(y1_ref[...]) + g2 * _unpack_halves(y2_ref[...])
    o_ref[...] = x_ref[...] + gate_ref[0] * _rms(y, gy_ref[...])


def _moe_finish(x, ypk, route, mod_row, gate, gy, *, tr=256):
    n, d = x.shape
    tr = _tile(n, tr, SUBLANE)
    nt = n // tr
    return pl.pallas_call(
        _moe_finish_kernel,
        out_shape=jax.ShapeDtypeStruct((n, d), F32),
        grid=(nt,),
        in_specs=[pl.BlockSpec((tr, d), lambda m: (m, 0)),
                  pl.BlockSpec((tr, d // 2), lambda m: (m, 0)),
                  pl.BlockSpec((tr, d // 2), lambda m: (nt + m, 0)),
                  pl.BlockSpec((tr, LANE), lambda m: (m, 0)),
                  pl.BlockSpec((1, 1, d), lambda m: (mod_row(m * tr), 0, 0)),
                  pl.BlockSpec((1, d), lambda m: (0, 0))],
        out_specs=pl.BlockSpec((tr, d), lambda m: (m, 0)),
        compiler_params=_cparams(("parallel",)),
        name="moe_finish",
    )(x, ypk, ypk, route, gate, gy.reshape(1, d))


def _ret_kernel(dec_ref, ql_ref, kl_ref, vl_ref, gl_ref, qc_ref, kc_ref, vc_ref, gc_ref, ca_ref, sa_ref,
                o_ref, qs, ks, kcs, af_l, ab_l, af_c, ab_c, st_f, st_b, *, chunk, kscale, unroll):
    hd = pl.program_id(1)
    s_len, dk = ql_ref.shape
    c_len = qc_ref.shape[0]
    dv = vl_ref.shape[1]
    ch = chunk
    ncl, ncc = s_len // ch, c_len // ch
    half = LANE // 2

    def sl(c):
        return pl.ds(pl.multiple_of(c * ch, ch), ch)

    lane = lax.broadcasted_iota(jnp.int32, (ch, LANE), 1)
    lo = lane < half
    sign = jnp.where(lo, -1.0, 1.0)

    def prep(c, _):
        r = sl(c)
        ca, sa = ca_ref[r, :], sa_ref[r, :]
        car, sar = pltpu.roll(ca, half, 1), pltpu.roll(sa, half, 1)
        cos = (jnp.where(lo, ca, car), jnp.where(lo, car, ca))
        sin = (sign * jnp.where(lo, sa, sar), sign * jnp.where(lo, sar, sa))

        def rope(x):
            parts = []
            for p in range(dk // LANE):
                xp = x[:, p * LANE:(p + 1) * LANE]
                parts.append(xp * cos[p] + pltpu.roll(xp, half, 1) * sin[p])
            return jnp.concatenate(parts, axis=-1)

        qs[r, :] = rope(ql_ref[r, :].astype(F32)).astype(BF16)
        ks[r, :] = (rope(kl_ref[r, :].astype(F32)) * kscale).astype(BF16)
        return 0

    lax.fori_loop(0, ncl, prep, 0)
    kcs[...] = (kc_ref[...].astype(F32) * kscale).astype(BF16)

    def log_gamma(dirn):
        d = jnp.full((1, 1), dec_ref[dirn, hd], F32)
        return -(jnp.maximum(-d, 0.0) + jnp.log1p(jnp.exp(-jnp.abs(d))))

    ii = lax.broadcasted_iota(jnp.int32, (ch, ch), 0).astype(F32)
    jj = lax.broadcasted_iota(jnp.int32, (ch, ch), 1).astype(F32)
    pos_k = lax.broadcasted_iota(jnp.int32, (ch, dk), 0).astype(F32)
    lg_f, lg_b = log_gamma(0), log_gamma(1)
    intra_f = jnp.where(ii >= jj, jnp.exp(lg_f * jnp.maximum(ii - jj, 0.0)), 0.0)
    intra_b = jnp.where(jj >= ii, jnp.exp(lg_b * jnp.maximum(jj - ii, 0.0)), 0.0)
    qdec_f = jnp.exp(lg_f * (pos_k + 1.0))
    kdec_f = jnp.exp(lg_f * (ch - 1.0 - pos_k))
    qdec_b = jnp.exp(lg_b * (ch - pos_k))
    kdec_b = jnp.exp(lg_b * pos_k)
    sdec_f = jnp.exp(lg_f * ch)
    sdec_b = jnp.exp(lg_b * ch)

    def step(q_src, k_src, v_src, acc, st, c, intra, qdec, kdec, sdec):
        r = sl(c)
        q, k, v = q_src[r, :], k_src[r, :], v_src[r, :]
        att = _dot_nt(q, k) * intra
        state = st[...]
        out = _dot(att.astype(BF16), v) + _dot((q.astype(F32) * qdec).astype(BF16), state.astype(BF16))
        acc[r, :] = out
        st[...] = state * sdec + _dot_tn((k.astype(F32) * kdec).astype(BF16), v)

    st_f[...] = jnp.zeros_like(st_f)
    st_b[...] = jnp.zeros_like(st_b)

    def scan(q_src, k_src, v_src, acc_f, acc_b, n):
        def body(c, _):
            step(q_src, k_src, v_src, acc_f, st_f, c, intra_f, qdec_f, kdec_f, sdec_f)
            step(q_src, k_src, v_src, acc_b, st_b, n - 1 - c, intra_b, qdec_b, kdec_b, sdec_b)
            return 0
        lax.fori_loop(0, n, body, 0, unroll=int(np.gcd(unroll, n)))

    scan(qc_ref, kcs, vc_ref, af_c, ab_c, ncc)
    scan(qs, ks, vl_ref, af_l, ab_l, ncl)

    def finish(acc_f, acc_b, g_ref, n, row0):
        def body(c, _):
            r = sl(c)
            y = acc_f[r, :] + acc_b[r, :]
            y = y * lax.rsqrt(jnp.mean(y * y, axis=-1, keepdims=True) + EPS)
            g = g_ref[r, :].astype(F32)
            dst = pl.ds(pl.multiple_of(row0 + c * ch, ch), ch)
            o_ref[dst, :] = (y * (g * jax.nn.sigmoid(g))).astype(o_ref.dtype)
            return 0
        lax.fori_loop(0, n, body, 0)

    finish(af_c, ab_c, gc_ref, ncc, 0)
    finish(af_l, ab_l, gl_ref, ncl, c_len)


def _retention(p0, ret_decay, rope_a, rope_b, b, s, c_len, dk, dv, *, unroll=8):
    nh = RET_HEADS
    assert dk == 2 * LANE and dv % LANE == 0 and s % RET_CHUNK == 0 and c_len % RET_CHUNK == 0
    assert s % c_len == 0
    cb0 = (b * s) // c_len
    assert dk == dv
    qo, ko, vo, go = 0, nh, 2 * nh, 3 * nh
    lat = lambda off: pl.BlockSpec((s, dk), lambda i, h, off=off: (i, off + h))
    ctx = lambda off: pl.BlockSpec((c_len, dk), lambda i, h, off=off: (cb0 + i, off + h))
    tab = pl.BlockSpec((s, LANE), lambda i, h: (0, 0))
    return pl.pallas_call(
        functools.partial(_ret_kernel, chunk=RET_CHUNK, kscale=dk ** -0.5, unroll=unroll),
        out_shape=jax.ShapeDtypeStruct((b * (s + c_len), nh * dv), BF16),
        grid=(b, nh),
        in_specs=[pl.BlockSpec(memory_space=pltpu.SMEM),
                  lat(qo), lat(ko), lat(vo), lat(go), ctx(qo), ctx(ko), ctx(vo), ctx(go), tab, tab],
        out_specs=pl.BlockSpec((s + c_len, dv), lambda i, h: (i, h)),
        scratch_shapes=[pltpu.VMEM((s, dk), BF16), pltpu.VMEM((s, dk), BF16), pltpu.VMEM((c_len, dk), BF16),
                        pltpu.VMEM((s, dv), F32), pltpu.VMEM((s, dv), F32),
                        pltpu.VMEM((c_len, dv), F32), pltpu.VMEM((c_len, dv), F32),
                        pltpu.VMEM((dk, dv), F32), pltpu.VMEM((dk, dv), F32)],
        compiler_params=_cparams(("parallel", "arbitrary")),
        name="retention",
    )(ret_decay.astype(F32), p0, p0, p0, p0, p0, p0, p0, p0, rope_a, rope_b)


def _na_kernel(q_ref, k_ref, v_ref, qc_ref, kc_ref, vc_ref, pt_ref, o_ref, *, qrows, krows, rows, w, scale):
    kh, g_rows = NA_KH, LANE // w
    nblk = rows // qrows
    kc, vc = kc_ref[...], vc_ref[...]
    c_len = kc_ref.shape[0]
    neg_tile = jnp.full((w, LANE), NEG_INF, F32)
    lane_row = lax.broadcasted_iota(jnp.int32, (w, LANE), 1) // w
    for blk in range(nblk):
        k0 = int(np.clip(blk * qrows - kh // 2, 0, rows - krows))
        bias_rows = []
        for i in range(qrows):
            qrow = blk * qrows + i
            win0 = int(np.clip(qrow - kh // 2, 0, rows - kh))
            tiles = []
            for jg in range(krows // g_rows):
                kr0 = k0 + jg * g_rows
                ok = [win0 <= kr0 + g < win0 + kh for g in range(g_rows)]
                if not any(ok):
                    tiles.append(neg_tile)
                    continue
                tile = pt_ref[0, kr0 - qrow + kh - 1 + g_rows - 1]
                if not all(ok):
                    lo, hi = ok.index(True), g_rows - ok[::-1].index(True)
                    tile = jnp.where((lane_row >= lo) & (lane_row < hi), tile, NEG_INF)
                tiles.append(tile)
            bias_rows.append(jnp.concatenate(tiles, axis=1))
        bias = jnp.concatenate(bias_rows, axis=0)
        qs = slice(blk * qrows * w, (blk + 1) * qrows * w)
        win = slice(k0 * w, (k0 + krows) * w)
        q = q_ref[qs, :]
        s_loc = _dot_nt(q, k_ref[win, :]) * scale + bias
        s_ctx = _dot_nt(q, kc) * scale
        m = jnp.maximum(jnp.max(s_loc, axis=-1, keepdims=True), jnp.max(s_ctx, axis=-1, keepdims=True))
        p_loc, p_ctx = jnp.exp(s_loc - m), jnp.exp(s_ctx - m)
        l = jnp.sum(p_loc, axis=-1, keepdims=True) + jnp.sum(p_ctx, axis=-1, keepdims=True)
        o = _dot(p_loc.astype(BF16), v_ref[win, :]) + _dot(p_ctx.astype(BF16), vc)
        o_ref[c_len + blk * qrows * w: c_len + (blk + 1) * qrows * w, :] = (o / l).astype(o_ref.dtype)

    sc = _dot_nt(qc_ref[...], kc) * scale
    pc = jnp.exp(sc - jnp.max(sc, axis=-1, keepdims=True))
    oc = _dot(pc.astype(BF16), vc) / jnp.sum(pc, axis=-1, keepdims=True)
    o_ref[0:c_len, :] = oc.astype(o_ref.dtype)


def _na_bias_tiles(rpb):
    w, kh, kw = GRID_W, NA_KH, NA_KW
    g_rows = LANE // w
    qc = np.arange(w)[:, None]
    kc = np.arange(w)[None, :]
    cstart = np.clip(qc - kw // 2, 0, w - kw)
    col_ok = (kc >= cstart) & (kc < cstart + kw)
    dc = np.clip(kc - qc + kw - 1, 0, 2 * kw - 2)
    per_dr = jnp.where(col_ok[None, None], rpb[:, :, dc], NEG_INF)
    n_a = 2 * kh - 1 + g_rows - 1
    idx = np.clip(np.arange(n_a)[:, None] - (g_rows - 1) + np.arange(g_rows)[None, :], 0, 2 * kh - 2)
    tiles = per_dr[:, idx]
    return jnp.transpose(tiles, (0, 1, 3, 2, 4)).reshape(rpb.shape[0], n_a, w, LANE).astype(F32)


def _neighbourhood(p0, rpb, b, s, c_len, hd, col0):
    nh, w, qr = NA_HEADS, GRID_W, NA_QROWS
    rows = s // w
    assert LANE % w == 0
    g_rows = LANE // w
    kr = -(-(qr + NA_KH - 1) // g_rows) * g_rows
    assert rows >= kr and rows % qr == 0 and rows // qr <= 16 and hd == LANE and s % c_len == 0
    cb0 = (b * s) // c_len
    qo, ko, vo = col0, col0 + nh, col0 + 2 * nh
    pt = _na_bias_tiles(rpb)
    lat = lambda off: pl.BlockSpec((s, hd), lambda i, h, off=off: (i, off + h))
    ctx = lambda off: pl.BlockSpec((c_len, hd), lambda i, h, off=off: (cb0 + i, off + h))
    return pl.pallas_call(
        functools.partial(_na_kernel, qrows=qr, krows=kr, rows=rows, w=w, scale=hd ** -0.5),
        out_shape=jax.ShapeDtypeStruct((b * (s + c_len), nh * hd), BF16),
        grid=(b, nh),
        in_specs=[lat(qo), lat(ko), lat(vo), ctx(qo), ctx(ko), ctx(vo),
                  pl.BlockSpec((1,) + pt.shape[1:], lambda i, h: (h, 0, 0, 0))],
        out_specs=pl.BlockSpec((s + c_len, hd), lambda i, h: (i, h)),
        compiler_params=_cparams(("parallel", "parallel")),
        name="neighbourhood",
    )(p0, p0, p0, p0, p0, p0, pt)


def _kpe_kernel(x_ref, t_ref, o_ref, *, n_lat):
    x = x_ref[...].astype(F32)
    half = LANE // 2
    lane = lax.broadcasted_iota(jnp.int32, x.shape, 1)
    r = x * t_ref[...]
    roped = r + pltpu.roll(r, half, 1)
    is_lat = pl.program_id(0) < n_lat
    o_ref[...] = jnp.where(lane < half, jnp.where(is_lat, roped, x), 0.0).astype(o_ref.dtype)


def _mla_kernel(qn_ref, qp_ref, tq_ref, kn_ref, kp_ref, v_ref, knc_ref, kpc_ref, vc_ref, o_ref, k_s, *, scale, tk):
    c_len, s_len = knc_ref.shape[0], kn_ref.shape[0]
    tq = qn_ref.shape[0]
    half = LANE // 2

    @pl.when(pl.program_id(2) == 0)
    def _():
        k_s[0:c_len, 0:LANE] = knc_ref[...]
        k_s[0:c_len, LANE:] = kpc_ref[...]
        k_s[c_len:, 0:LANE] = kn_ref[...]
        k_s[c_len:, LANE:] = kp_ref[...]

    c = scale * np.log2(np.e)
    r = qp_ref[...].astype(F32) * tq_ref[...]
    qpe = r + pltpu.roll(r, half, 1)
    q = jnp.concatenate([(qn_ref[...].astype(F32) * c).astype(BF16), (qpe * c).astype(BF16)], axis=-1)

    def update(k, v, carry):
        m, l, acc = carry
        s = _dot_nt(q, k)
        m_new = jnp.maximum(m, jnp.max(s, axis=-1, keepdims=True))
        alpha = jnp.exp2(m - m_new)
        p = jnp.exp2(s - jnp.tile(m_new, (1, s.shape[1] // LANE)))
        l = alpha * l + jnp.sum(p, axis=-1, keepdims=True)
        acc = alpha * acc + _dot(p.astype(BF16), v)
        return m_new, l, acc

    carry = (jnp.full((tq, LANE), NEG_INF, F32), jnp.zeros((tq, LANE), F32), jnp.zeros((tq, LANE), F32))
    carry = update(k_s[0:c_len, :], vc_ref[...], carry)
    for j in range(s_len // tk):
        carry = update(k_s[c_len + j * tk: c_len + (j + 1) * tk, :], v_ref[j * tk:(j + 1) * tk, :], carry)
    _, l, acc = carry
    o_ref[...] = (acc / l).astype(o_ref.dtype)


def _mla_attention(q, kv, kpe, tab, b, s, c_len, *, tq=512, tk=2048):
    nh = MLA_HEADS
    assert MLA_NOPE == LANE and MLA_V == LANE and 2 * MLA_ROPE == LANE and c_len % LANE == 0
    tq, tk = _tile(s, tq, SUBLANE), _tile(s, tk)
    nq = s // tq
    cb0 = (b * s) // c_len
    return pl.pallas_call(
        functools.partial(_mla_kernel, scale=(MLA_NOPE + MLA_ROPE) ** -0.5, tk=tk),
        out_shape=jax.ShapeDtypeStruct((b * s, nh * MLA_V), BF16),
        grid=(b, nh, nq),
        in_specs=[pl.BlockSpec((tq, LANE), lambda i, h, a: (i * nq + a, h)),
                  pl.BlockSpec((tq, LANE), lambda i, h, a: (i * nq + a, nh + h)),
                  pl.BlockSpec((tq, LANE), lambda i, h, a: (a, 0)),
                  pl.BlockSpec((s, LANE), lambda i, h, a: (i, h)),
                  pl.BlockSpec((s, LANE), lambda i, h, a: (i, 0)),
                  pl.BlockSpec((s, LANE), lambda i, h, a: (i, nh + h)),
                  pl.BlockSpec((c_len, LANE), lambda i, h, a: (cb0 + i, h)),
                  pl.BlockSpec((c_len, LANE), lambda i, h, a: (cb0 + i, 0)),
                  pl.BlockSpec((c_len, LANE), lambda i, h, a: (cb0 + i, nh + h))],
        out_specs=pl.BlockSpec((tq, MLA_V), lambda i, h, a: (i * nq + a, h)),
        scratch_shapes=[pltpu.VMEM((c_len + s, 2 * LANE), BF16)],
        compiler_params=_cparams(("parallel", "parallel", "arbitrary")),
        name="mla_attention",
    )(q, q, tab, kv, kpe, kv, kv, kpe, kv)


def _lru_kernel(xl_ref, xc_ref, yl_ref, cw_ref, cb_ref, wa_ref, wi_ref, ba_ref, bi_ref, lam_ref, o_ref,
                xp_l, xp_c, a_f, u_f, a_b, u_b, *, rows_per_step):
    s_len, c_len = xl_ref.shape[0], xc_ref.shape[0]
    pad = SUBLANE
    left = (LRU_CONV - 1) // 2
    cw, cb = cw_ref[...], cb_ref[...]

    def softplus(x):
        return jnp.maximum(x, 0.0) + jnp.log1p(jnp.exp(-jnp.abs(x)))

    sp = [softplus(-lam_ref[d:d + 1, :]) for d in range(2)]

    def gates(xp, n, f_off, b_off):
        step = min(rows_per_step, n)
        for r0 in range(0, n, step):
            x = cb
            for j in range(LRU_CONV):
                x = x + cw[j:j + 1, :] * xp[pad - left + j + r0: pad - left + j + r0 + step, :]
            xb = x.astype(BF16)

            def blockdiag(w_ref, d):
                return jnp.concatenate([_dot(xb[:, g * LANE:(g + 1) * LANE], w_ref[d, g])
                                        for g in range(w_ref.shape[1])], axis=-1)

            for d, (a_ref, u_ref, off) in enumerate(((a_f, u_f, f_off), (a_b, u_b, b_off))):
                rg = _sigmoid(blockdiag(wa_ref, d) + ba_ref[d:d + 1, :])
                ig = _sigmoid(blockdiag(wi_ref, d) + bi_ref[d:d + 1, :])
                log_a = -LRU_C * rg * sp[d]
                a_ref[off + r0: off + r0 + step, :] = jnp.exp(log_a)
                th = jnp.tanh(log_a)
                u_ref[off + r0: off + r0 + step, :] = jnp.sqrt(-2.0 * th / (1.0 - th)) * ig * x

    zeros = jnp.zeros((pad, xl_ref.shape[1]), F32)
    for xp, src, n in ((xp_l, xl_ref, s_len), (xp_c, xc_ref, c_len)):
        xp[0:pad, :] = zeros
        xp[pad + n: 2 * pad + n, :] = zeros
        xp[pad: pad + n, :] = src[...].astype(F32)
    gates(xp_c, c_len, 0, s_len)
    gates(xp_l, s_len, c_len, 0)

    n_tiles = (s_len + c_len) // SUBLANE
    row = lax.broadcasted_iota(jnp.int32, (SUBLANE, xl_ref.shape[1]), 0)

    def body(i, carry):
        hf, hb = carry
        rf = pl.ds(pl.multiple_of(i * SUBLANE, SUBLANE), SUBLANE)
        a, u = a_f[rf, :], u_f[rf, :]
        for sft in (1, 2, 4):
            m = row >= sft
            u = jnp.where(m, a * pltpu.roll(u, sft, 0) + u, u)
            a = jnp.where(m, a * pltpu.roll(a, sft, 0), a)
        h = a * hf + u
        u_f[rf, :] = h
        hf = h[SUBLANE - 1:SUBLANE, :]
        rb = pl.ds(pl.multiple_of((n_tiles - 1 - i) * SUBLANE, SUBLANE), SUBLANE)
        a, u = a_b[rb, :], u_b[rb, :]
        for sft in (1, 2, 4):
            m = row < SUBLANE - sft
            u = jnp.where(m, a * pltpu.roll(u, SUBLANE - sft, 0) + u, u)
            a = jnp.where(m, a * pltpu.roll(a, SUBLANE - sft, 0), a)
        h = a * hb + u
        u_b[rb, :] = h
        hb = h[0:1, :]
        return hf, hb

    h0 = jnp.zeros((1, xl_ref.shape[1]), F32)
    lax.fori_loop(0, n_tiles, body, (h0, h0), unroll=4)

    step = min(rows_per_step, s_len)
    for r0 in range(0, s_len, step):
        y = yl_ref[r0:r0 + step, :].astype(F32)
        h = u_f[c_len + r0: c_len + r0 + step, :] + u_b[r0:r0 + step, :]
        o_ref[r0:r0 + step, :] = (h * jax.nn.gelu(y)).astype(o_ref.dtype)


def _rglru(p1, conv_w, conv_b, wa, wi, ba, bi, lam, b, s, c_len, x_col0, y_col0, *, group=2):
    nblk, bd = wa.shape[1], wa.shape[2]
    width = nblk * bd
    group = int(np.gcd(np.gcd(group, nblk), np.gcd(x_col0, y_col0)))
    gw = group * bd
    assert bd == LANE and s % c_len == 0 and (s + c_len) % (4 * SUBLANE) == 0
    cb0 = (b * s) // c_len
    t = s + c_len
    xc, yc = x_col0 // group, y_col0 // group
    vec2 = pl.BlockSpec((2, gw), lambda i, k: (0, k))
    wspec = pl.BlockSpec((2, group, bd, bd), lambda i, k: (0, k, 0, 0))
    return pl.pallas_call(
        functools.partial(_lru_kernel, rows_per_step=512),
        out_shape=jax.ShapeDtypeStruct((b * s, width), BF16),
        grid=(b, nblk // group),
        in_specs=[pl.BlockSpec((s, gw), lambda i, k: (i, xc + k)),
                  pl.BlockSpec((c_len, gw), lambda i, k: (cb0 + i, xc + k)),
                  pl.BlockSpec((s, gw), lambda i, k: (i, yc + k)),
                  pl.BlockSpec((LRU_CONV, gw), lambda i, k: (0, k)),
                  pl.BlockSpec((1, gw), lambda i, k: (0, k)),
                  wspec, wspec, vec2, vec2, vec2],
        out_specs=pl.BlockSpec((s, gw), lambda i, k: (i, k)),
        scratch_shapes=[pltpu.VMEM((s + 2 * SUBLANE, gw), F32), pltpu.VMEM((c_len + 2 * SUBLANE, gw), F32),
                        pltpu.VMEM((t, gw), F32), pltpu.VMEM((t, gw), F32),
                        pltpu.VMEM((t, gw), F32), pltpu.VMEM((t, gw), F32)],
        compiler_params=_cparams(("parallel", "parallel")),
        name="rglru",
    )(p1, p1, p1, conv_w.astype(F32), conv_b.reshape(1, width).astype(F32),
      wa.astype(BF16), wi.astype(BF16), ba.astype(F32), bi.astype(F32), lam.astype(F32))


def _rope_angles(n, rot_dim):
    t = np.arange(n)
    row = (t // GRID_W).astype(np.float32)
    col = (t % GRID_W).astype(np.float32)
    nf = rot_dim // 4
    inv = jnp.asarray(ROPE_BASE, F32) ** (-jnp.arange(nf, dtype=F32) / nf)
    return row[:, None] * inv, col[:, None] * inv


def _swap_perm(rot_dim):
    q = rot_dim // 4
    idx = np.arange(rot_dim).reshape(4, q)
    return idx[[1, 0, 3, 2]].reshape(-1)


def kernel(x, c, ctx, c_ctx, w_mod, b_mod, norm_g, w_in_even, w_out_even, ret_decay, na_rpb, ffn_w_gate, ffn_w_up,
           ffn_w_down, w_in_odd, mla_q_norm, mla_w_uq, mla_kv_norm, mla_w_ukv, lru_conv_w, lru_conv_b, lru_w_a,
           lru_b_a, lru_w_i, lru_b_i, lru_lambda, w_out_odd, router_w, moe_w_gate, moe_w_up, moe_w_down):
    b, s, d = x.shape
    c_len = ctx.shape[1]
    n_lat, n_ctx = b * s, b * c_len
    n_tok = n_lat + n_ctx
    depth = w_mod.shape[0]
    assert depth == 2 and w_in_even.shape[0] == 1 and w_in_odd.shape[0] == 1
    assert n_ctx % SUBLANE == 0 and s % c_len == 0

    mod_rows = -(-(b + 1) // SUBLANE) * SUBLANE
    s_in = jnp.concatenate([c, c_ctx[None], jnp.zeros((mod_rows - b - 1, d), F32)], axis=0)
    mods = _modulation(s_in, w_mod, b_mod).reshape(depth, mod_rows, 6, 1, d)
    mod = lambda l, j: mods[l, :, j]
    mod_row = lambda r0: jnp.where(r0 < n_lat, r0 // s, b)


    g = norm_g[0]
    (h,) = _resnorm(x, n_tok, mod_row, x_ctx=ctx, gn=g[0], sc=mod(0, 0), sh=mod(0, 1))
    w_in = w_in_even[0].astype(BF16)
    p0 = _mm([(h, d, 0)], [(w_in, d, 0)], n_tok, w_in.shape[1], BF16, tn=1024, name="in_proj_even")
    ret_w = w_out_even.shape[1] - NA_HEADS * LANE
    dk = ret_w // RET_HEADS
    ar, ac = _rope_angles(s, dk)
    rope_a = jnp.concatenate([jnp.cos(ar), jnp.cos(ac)], axis=-1)
    rope_b = jnp.concatenate([jnp.sin(ar), jnp.sin(ac)], axis=-1)
    mix_a = _retention(p0, ret_decay[0], rope_a, rope_b, b, s, c_len, dk, dk)
    mix_b = _neighbourhood(p0, na_rpb[0], b, s, c_len, LANE, 4 * ret_w // LANE)
    w_out = w_out_even[0].astype(BF16)
    half_k = w_out.shape[0] // 2
    assert mix_a.shape[1] == half_k and mix_b.shape[1] == half_k
    o = _mm([(mix_a, half_k, 0), (mix_b, half_k, 0)], [(w_out, half_k, 0), (w_out, half_k, 1)],
            n_tok, d, BF16, tn=1024, name="out_proj_even")
    xs, h = _resnorm(x, n_tok, mod_row, x_ctx=ctx, y=o, y_batch_major=True, gate=mod(0, 2), gy=g[1], gn=g[2],
                     sc=mod(0, 3), sh=mod(0, 4))
    y = _ffn(h, ffn_w_gate[0].astype(BF16), ffn_w_up[0].astype(BF16), ffn_w_down[0].astype(BF16),
             tm=1024, single=1)
    g1 = norm_g[1]
    xs, h = _resnorm(xs, n_tok, mod_row, y=y, gate=mod(0, 5), gy=g[3], gn=g1[0], sc=mod(1, 0), sh=mod(1, 1))

    g = g1
    qr, kvr, rr = MLA_Q_RANK, MLA_KV_RANK, MLA_ROPE
    lw = lru_w_a.shape[2] * lru_w_a.shape[3]
    perm = _swap_perm(rr)
    wi = w_in_odd[0]
    o_q, o_kv, o_kr, o_lx, o_ly = 0, qr, qr + kvr, qr + kvr + rr, qr + kvr + rr + lw
    kr_w = wi[:, o_kr:o_kr + rr]
    w_in = jnp.concatenate([wi[:, o_q:o_q + qr], wi[:, o_kv:o_kv + kvr], wi[:, o_lx:o_lx + lw],
                            wi[:, o_ly:o_ly + lw], kr_w, kr_w[:, perm]], axis=-1).astype(BF16)
    assert qr % LANE == 0 and kvr % LANE == 0 and lw % LANE == 0 and 2 * rr == LANE
    p1 = _mm([(h, d, 0)], [(w_in, d, 0)], n_tok, w_in.shape[1], BF16, tn=896, name="in_proj_odd")
    nh = MLA_HEADS
    wq = mla_w_uq[0].reshape(qr, nh, MLA_NOPE + rr)
    wq_pe = wq[:, :, MLA_NOPE:]
    wq = jnp.concatenate([wq[:, :, :MLA_NOPE].reshape(qr, nh * MLA_NOPE),
                          jnp.concatenate([wq_pe, wq_pe[:, :, perm]], axis=-1).reshape(qr, nh * LANE)],
                         axis=-1).astype(BF16)
    wkv = mla_w_ukv[0].reshape(kvr, nh, MLA_NOPE + MLA_V)
    wkv = jnp.concatenate([wkv[:, :, :MLA_NOPE].reshape(kvr, nh * MLA_NOPE),
                           wkv[:, :, MLA_NOPE:].reshape(kvr, nh * MLA_V)], axis=-1).astype(BF16)
    assert qr % kvr == 0
    q = _mm([(p1, qr, 0)], [(wq, qr, 0)], n_lat, wq.shape[1], BF16, norm_g=mla_q_norm[0], tn=1024, name="mla_q")
    kv = _mm([(p1, kvr, qr // kvr)], [(wkv, kvr, 0)], n_tok, wkv.shape[1], BF16, norm_g=mla_kv_norm[0],
             name="mla_kv")
    ar, ac = _rope_angles(s, rr)
    cos4 = jnp.concatenate([jnp.cos(ar), jnp.cos(ar), jnp.cos(ac), jnp.cos(ac)], axis=-1)
    sin4 = jnp.concatenate([-jnp.sin(ar), jnp.sin(ar), -jnp.sin(ac), jnp.sin(ac)], axis=-1)
    tab = jnp.concatenate([cos4, sin4], axis=-1)
    kr_cb = (qr + kvr + 2 * lw) // LANE
    tr = _tile(s, 512, SUBLANE)
    assert n_ctx % tr == 0 or tr % n_ctx == 0
    tr = min(tr, n_ctx) if n_ctx % tr else tr
    nt_lat, nt_seq = n_lat // tr, s // tr
    kpe = pl.pallas_call(
        functools.partial(_kpe_kernel, n_lat=nt_lat),
        out_shape=jax.ShapeDtypeStruct((n_tok, LANE), BF16),
        grid=(n_tok // tr,),
        in_specs=[pl.BlockSpec((tr, LANE), lambda m: (m, kr_cb)),
                  pl.BlockSpec((tr, LANE), lambda m: (jnp.where(m < nt_lat, m % nt_seq, 0), 0))],
        out_specs=pl.BlockSpec((tr, LANE), lambda m: (m, 0)),
        compiler_params=_cparams(("parallel",)),
        name="mla_kpe",
    )(p1, tab)
    att = _mla_attention(q, kv, kpe, tab, b, s, c_len)
    lru = _rglru(p1, lru_conv_w[0], lru_conv_b[0], lru_w_a[0], lru_w_i[0], lru_b_a[0], lru_b_i[0], lru_lambda[0],
                 b, s, c_len, (qr + kvr) // LANE, (qr + kvr + lw) // LANE)
    w_out = w_out_odd[0].astype(BF16)
    half_k = w_out.shape[0] // 2
    assert att.shape[1] == half_k and lru.shape[1] == half_k
    o = _mm([(att, half_k, 0), (lru, half_k, 0)], [(w_out, half_k, 0), (w_out, half_k, 1)],
            n_lat, d, BF16, tn=1024, name="out_proj_odd")
    xl, hpk, route = _resnorm(xs, n_lat, mod_row, y=o, gate=mod(1, 2), gy=g[1], gn=g[2], sc=mod(1, 3),
                              sh=mod(1, 4), router_w=router_w[0])
    ypk = _moe(hpk, route, moe_w_gate[0].astype(BF16), moe_w_up[0].astype(BF16), moe_w_down[0].astype(BF16))
    xl = _moe_finish(xl, ypk, route, mod_row, mod(1, 5), g[3])
    return xl.reshape(b, s, d)
```

```python
import functools

import numpy as np
import jax
import jax.numpy as jnp
from jax import lax
from jax.experimental import pallas as pl
from jax.experimental.pallas import tpu as pltpu

F32 = jnp.float32
BF16 = jnp.bfloat16

EPS = 1e-6
NEG_INF = -1e30
ROPE_BASE = 10000.0
GRID_W = 64
RET_HEADS = 8
RET_CHUNK = 128
NA_HEADS = 16
NA_KH = 8
NA_KW = 16
NA_QROWS = 8
MLA_HEADS = 16
MLA_Q_RANK = 1536
MLA_KV_RANK = 512
MLA_NOPE = 128
MLA_ROPE = 64
MLA_V = 128
LRU_C = 8.0
LRU_CONV = 4
TOP_K = 2

LANE = 128
SUBLANE = 8
VMEM_LIMIT = 52 * 1024 * 1024


def _cparams(sem):
    return pltpu.CompilerParams(dimension_semantics=sem, vmem_limit_bytes=VMEM_LIMIT)


def _tile(n, target, mult=LANE):
    if n <= target:
        return n
    best = None
    for t in range(mult, target + 1, mult):
        if n % t == 0:
            best = t
    assert best is not None, (n, target, mult)
    return best


def _dot(a, b):
    return jnp.dot(a, b, preferred_element_type=F32)


def _dot_nt(a, b):
    return lax.dot_general(a, b, (((1,), (1,)), ((), ())), preferred_element_type=F32)


def _dot_tn(a, b):
    return lax.dot_general(a, b, (((0,), (0,)), ((), ())), preferred_element_type=F32)


def _sigmoid(x):
    return 0.5 * jnp.tanh(0.5 * x) + 0.5


def _rms(x, g):
    return x * lax.rsqrt(jnp.mean(x * x, axis=-1, keepdims=True) + EPS) * g


def _mod_kernel(s_ref, w_ref, b_ref, o_ref):
    s = s_ref[...]
    s = (s * jax.nn.sigmoid(s)).astype(BF16)
    o_ref[0] = _dot(s, w_ref[0].astype(BF16)) + b_ref[0]


def _modulation(s_in, w_mod, b_mod):
    nl, d, n = w_mod.shape
    rows = s_in.shape[0]
    tn = _tile(n, 512)
    return pl.pallas_call(
        _mod_kernel,
        out_shape=jax.ShapeDtypeStruct((nl, rows, n), F32),
        grid=(nl, n // tn),
        in_specs=[pl.BlockSpec((rows, d), lambda l, j: (0, 0)),
                  pl.BlockSpec((1, d, tn), lambda l, j: (l, 0, j)),
                  pl.BlockSpec((1, 1, tn), lambda l, j: (l, 0, j))],
        out_specs=pl.BlockSpec((1, rows, tn), lambda l, j: (l, 0, j)),
        compiler_params=_cparams(("arbitrary", "arbitrary")),
        name="modulation",
    )(s_in, w_mod, b_mod.reshape(nl, 1, n))


def _pack_halves(v):
    half = v.shape[1] // 2
    lo = lax.bitcast_convert_type(v[:, :half].astype(BF16).astype(F32), jnp.uint32) >> 16
    hi = lax.bitcast_convert_type(v[:, half:].astype(BF16).astype(F32), jnp.uint32) & jnp.uint32(0xFFFF0000)
    return lo | hi


def _unpack_halves(u):
    lo = lax.bitcast_convert_type(u << 16, F32)
    hi = lax.bitcast_convert_type(u & jnp.uint32(0xFFFF0000), F32)
    return jnp.concatenate([lo, hi], axis=-1)


def _route_top2(logits, n_exp):
    lane = lax.broadcasted_iota(jnp.int32, logits.shape, 1)
    big = jnp.int32(2 ** 30)
    logits = jnp.where(lane < n_exp, logits, -jnp.inf)
    m1 = jnp.max(logits, axis=-1, keepdims=True)
    i1 = jnp.min(jnp.where(logits == m1, lane, big), axis=-1, keepdims=True)
    rest = jnp.where(lane == i1, -jnp.inf, logits)
    m2 = jnp.max(rest, axis=-1, keepdims=True)
    i2 = jnp.min(jnp.where(rest == m2, lane, big), axis=-1, keepdims=True)
    e2 = jnp.exp(m2 - m1)
    g1 = 1.0 / (1.0 + e2)
    g2 = e2 / (1.0 + e2)
    out = jnp.where(lane == 0, i1.astype(F32), 0.0) + jnp.where(lane == 1, i2.astype(F32), 0.0)
    return out + jnp.where(lane == 2, g1, 0.0) + jnp.where(lane == 3, g2, 0.0)


def _resnorm_kernel(*refs, has_y, has_next, n_exp, lat_tiles):
    it = iter(refs)
    x_ref = next(it)
    xc_ref = next(it) if lat_tiles else None
    if has_y:
        y_ref, gate_ref, gy_ref = next(it), next(it), next(it)
    if has_next:
        gn_ref, sc_ref, sh_ref = next(it), next(it), next(it)
    if n_exp:
        rw_ref = next(it)
    if lat_tiles:
        x = jnp.where(pl.program_id(0) < lat_tiles, x_ref[0], xc_ref[0])
    else:
        x = x_ref[...]
    if has_y:
        xo_ref = next(it)
        x = x + gate_ref[0] * _rms(y_ref[...].astype(F32), gy_ref[...])
        xo_ref[...] = x
    if has_next:
        h_ref = next(it)
        h = _rms(x, gn_ref[...]) * (1.0 + sc_ref[0]) + sh_ref[0]
        if n_exp:
            h_ref[...] = _pack_halves(h)
            route_ref = next(it)
            logits = jnp.dot(h, rw_ref[...], preferred_element_type=F32, precision=lax.Precision.HIGHEST)
            route_ref[...] = _route_top2(logits, n_exp)
        else:
            h_ref[...] = h.astype(BF16)


def _resnorm(x, nrows, mod_row, *, x_ctx=None, y=None, y_batch_major=False, gate=None, gy=None, gn=None, sc=None,
             sh=None, router_w=None, tr=256):
    d = x.shape[-1]
    tr = _tile(nrows, tr, SUBLANE)
    has_y, has_next = y is not None, gn is not None
    n_exp = 0 if router_w is None else router_w.shape[1]
    row = pl.BlockSpec((tr, d), lambda m: (m, 0))
    vec = pl.BlockSpec((1, d), lambda m: (0, 0))
    modv = pl.BlockSpec((1, 1, d), lambda m: (mod_row(m * tr), 0, 0))
    args, specs, outs, ospecs = [x], [row], [], []
    lat_tiles = 0
    if x_ctx is not None:
        (_, s_len, _), c_len = x.shape, x_ctx.shape[1]
        tr = _tile(int(np.gcd(s_len, c_len)), tr, SUBLANE)
        spt, cpt = s_len // tr, c_len // tr
        lat_tiles = x.shape[0] * spt
        lat = lambda m: jnp.minimum(m, lat_tiles - 1)
        cx = lambda m: jnp.maximum(m - lat_tiles, 0)
        args = [x, x_ctx]
        specs = [pl.BlockSpec((1, tr, d), lambda m: (lat(m) // spt, lat(m) % spt, 0)),
                 pl.BlockSpec((1, tr, d), lambda m: (cx(m) // cpt, cx(m) % cpt, 0))]
        row = pl.BlockSpec((tr, d), lambda m: (m, 0))
        modv = pl.BlockSpec((1, 1, d), lambda m: (mod_row(m * tr), 0, 0))
    yrow = row
    if y_batch_major:
        per = spt + cpt
        yrow = pl.BlockSpec((tr, d), lambda m: (jnp.where(m < lat_tiles, (m // spt) * per + cpt + m % spt,
                                                          (cx(m) // cpt) * per + cx(m) % cpt), 0))
    if has_y:
        args += [y, gate, gy.reshape(1, d)]
        specs += [yrow, modv, vec]
        outs.append(jax.ShapeDtypeStruct((nrows, d), F32))
        ospecs.append(row)
    if has_next:
        args += [gn.reshape(1, d), sc, sh]
        specs += [vec, modv, modv]
        if n_exp:
            assert n_exp <= LANE and TOP_K == 2
            args.append(jnp.pad(router_w.astype(F32), ((0, 0), (0, LANE - n_exp))))
            specs.append(pl.BlockSpec((d, LANE), lambda m: (0, 0)))
            outs += [jax.ShapeDtypeStruct((nrows, d // 2), jnp.uint32), jax.ShapeDtypeStruct((nrows, LANE), F32)]
            ospecs += [pl.BlockSpec((tr, d // 2), lambda m: (m, 0)), pl.BlockSpec((tr, LANE), lambda m: (m, 0))]
        else:
            outs.append(jax.ShapeDtypeStruct((nrows, d), BF16))
            ospecs.append(row)
    res = pl.pallas_call(
        functools.partial(_resnorm_kernel, has_y=has_y, has_next=has_next, n_exp=n_exp, lat_tiles=lat_tiles),
        out_shape=outs, grid=(nrows // tr,), in_specs=specs, out_specs=ospecs,
        compiler_params=_cparams(("parallel",)),
        name="resnorm",
    )(*args)
    return res


def _mm_kernel(*refs, nparts, has_norm):
    x_refs, w_refs = refs[:nparts], refs[nparts:2 * nparts]
    pos = 2 * nparts
    if has_norm:
        g_ref, o_ref, h_ref = refs[pos], refs[pos + 1], refs[pos + 2]

        @pl.when(pl.program_id(1) == 0)
        def _():
            h_ref[...] = _rms(x_refs[0][...].astype(F32), g_ref[...]).astype(BF16)

        acc = _dot(h_ref[...], w_refs[0][...])
    else:
        o_ref = refs[pos]
        acc = _dot(x_refs[0][...], w_refs[0][...])
        for i in range(1, nparts):
            acc = acc + _dot(x_refs[i][...], w_refs[i][...])
    o_ref[...] = acc.astype(o_ref.dtype)


def _mm(xs, ws, m, n, out_dtype, *, norm_g=None, tm=1024, tn=512, name="mm"):
    tm, tn = _tile(m, tm, SUBLANE), _tile(n, tn)
    has_norm = norm_g is not None
    args, specs = [], []
    for (a, k, cb) in xs:
        args.append(a)
        specs.append(pl.BlockSpec((tm, k), lambda i, j, cb=cb: (i, cb)))
    for (w, k, rb) in ws:
        args.append(w)
        specs.append(pl.BlockSpec((k, tn), lambda i, j, rb=rb: (rb, j)))
    scratch = []
    if has_norm:
        k0 = xs[0][1]
        args.append(norm_g.reshape(1, k0).astype(F32))
        specs.append(pl.BlockSpec((1, k0), lambda i, j: (0, 0)))
        scratch.append(pltpu.VMEM((tm, k0), BF16))
    return pl.pallas_call(
        functools.partial(_mm_kernel, nparts=len(xs), has_norm=has_norm),
        out_shape=jax.ShapeDtypeStruct((m, n), out_dtype),
        grid=(m // tm, n // tn), in_specs=specs,
        out_specs=pl.BlockSpec((tm, tn), lambda i, j: (i, j)),
        scratch_shapes=scratch,
        compiler_params=_cparams(("parallel", "arbitrary")),
        name=name,
    )(*args)


def _swiglu_step(h, wg, wu, wd):
    g = _dot(h, wg)
    u = _dot(h, wu)
    return _dot((g * jax.nn.sigmoid(g) * u).astype(BF16), wd)


def _ffn_kernel(h_ref, wg_ref, wu_ref, wd_ref, o_ref, acc):
    f = pl.program_id(1)

    @pl.when(f == 0)
    def _():
        acc[...] = jnp.zeros_like(acc)

    acc[...] += _swiglu_step(h_ref[...], wg_ref[...], wu_ref[...], wd_ref[...])

    @pl.when(f == pl.num_programs(1) - 1)
    def _():
        o_ref[...] = acc[...].astype(o_ref.dtype)


def _ffn(h, wg, wu, wd, *, tm=1024, tf=256):
    m, d = h.shape
    f = wg.shape[1]
    tm, tf = _tile(m, tm, SUBLANE), _tile(f, tf)
    once = pl.Buffered(1)
    return pl.pallas_call(
        _ffn_kernel,
        out_shape=jax.ShapeDtypeStruct((m, d), BF16),
        grid=(m // tm, f // tf),
        in_specs=[pl.BlockSpec((tm, d), lambda i, j: (i, 0), pipeline_mode=once),
                  pl.BlockSpec((d, tf), lambda i, j: (0, j)),
                  pl.BlockSpec((d, tf), lambda i, j: (0, j)),
                  pl.BlockSpec((tf, d), lambda i, j: (j, 0))],
        out_specs=pl.BlockSpec((tm, d), lambda i, j: (i, 0), pipeline_mode=once),
        scratch_shapes=[pltpu.VMEM((tm, d), F32)],
        compiler_params=_cparams(("parallel", "arbitrary")),
        name="ffn",
    )(h, wg, wu, wd)


def _moe_kernel(te_ref, nu_ref, src_ref, aid_ref, h_hbm, wg_ref, wu_ref, wd_ref, y_hbm,
                hbuf, hb, acc, stage, gsem, ssem, *, nf, p_rows):
    i, f = pl.program_id(0), pl.program_id(1)
    nt = pl.num_programs(0)
    tm = hb.shape[0]
    rows = tm // nf
    n_used = nu_ref[0]
    used = i < n_used
    slot = i % 2
    r0 = f * rows

    def gather_row(tile, sl, r):
        tok = src_ref[tile * tm + r]
        pltpu.make_async_copy(h_hbm.at[pl.ds(tok, 1)], hbuf.at[sl, pl.ds(r, 1)], gsem.at[sl]).start()

    def scatter_row(sl, r, dst):
        pltpu.make_async_copy(stage.at[sl, pl.ds(r, 1)], y_hbm.at[pl.ds(dst, 1)], ssem.at[sl]).start()

    def tile_done(buf, sem):
        pltpu.make_async_copy(buf, buf, sem).wait()

    def scatter_slice():
        prev = jnp.maximum(i - 1, 0)
        for r in range(rows):
            dst = jnp.where(i >= 1, aid_ref[prev * tm + r0 + r], p_rows + r0 + r)
            scatter_row(1 - slot, r0 + r, dst)

    @pl.when(jnp.logical_and(f == 0, i == 0))
    def _():
        stage[1] = jnp.zeros(stage.shape[1:], stage.dtype)

        def body(r, _):
            gather_row(0, 0, r)
            return 0
        lax.fori_loop(0, tm, body, 0, unroll=8)

    @pl.when(jnp.logical_and(f == 0, i <= n_used))
    def _():
        tile_done(hbuf.at[slot], gsem.at[slot])

    @pl.when(jnp.logical_and(f == 0, used))
    def _():
        hb[...] = _unpack_halves(hbuf[slot]).astype(BF16)
        acc[...] = jnp.zeros_like(acc)

    @pl.when(used)
    def _():
        for r in range(rows):
            gather_row(i + 1, 1 - slot, r0 + r)
        scatter_slice()
        acc[...] += _swiglu_step(hb[...], wg_ref[0], wu_ref[0], wd_ref[0])

    @pl.when(jnp.logical_not(used))
    def _():
        scatter_slice()

    @pl.when(f == nf - 1)
    def _():
        @pl.when(i >= 1)
        def _():
            tile_done(stage.at[slot], ssem.at[slot])

        @pl.when(used)
        def _():
            stage[slot] = _pack_halves(acc[...])

        @pl.when(jnp.logical_not(used))
        def _():
            stage[slot] = jnp.zeros(stage.shape[1:], stage.dtype)

        @pl.when(i == nt - 1)
        def _():
            def body(r, _):
                scatter_row(slot, r, aid_ref[i * tm + r])
                return 0
            lax.fori_loop(0, tm, body, 0, unroll=8)
            tile_done(stage.at[slot], ssem.at[slot])
            tile_done(stage.at[1 - slot], ssem.at[1 - slot])


def _moe_route(route, n_exp, tm):
    n = route.shape[0]
    e = route[:, :TOP_K].astype(jnp.int32).T.reshape(-1)
    onehot = (e[:, None] == jnp.arange(n_exp, dtype=jnp.int32)[None, :]).astype(jnp.int32)
    rank = jnp.sum((jnp.cumsum(onehot, axis=0) - onehot) * onehot, axis=1)
    counts = jnp.sum(onehot, axis=0)
    padded = ((counts + tm - 1) // tm) * tm
    ends = jnp.cumsum(padded)
    starts = ends - padded
    dest = starts[e] + rank
    p_rows = TOP_K * n + n_exp * tm
    n_tiles = p_rows // tm
    aid = jnp.full((p_rows,), -1, jnp.int32).at[dest].set(jnp.arange(TOP_K * n, dtype=jnp.int32))
    is_pad = aid < 0
    src = jnp.where(is_pad, 0, aid % n)
    aid = jnp.where(is_pad, TOP_K * n + jnp.cumsum(is_pad.astype(jnp.int32)) - 1, aid)
    n_used = (ends[-1] // tm).astype(jnp.int32)
    tile0 = jnp.arange(n_tiles, dtype=jnp.int32) * tm
    te = jnp.minimum(jnp.sum((tile0[:, None] >= ends[None, :]).astype(jnp.int32), axis=1), n_exp - 1)
    te = jnp.where(jnp.arange(n_tiles) < n_used, te, te[jnp.maximum(n_used - 1, 0)])
    return te, n_used.reshape(1), src, aid


def _moe(hpk, route, wg, wu, wd, *, tm=512, tf=256):
    n, dh = hpk.shape
    d = 2 * dh
    n_exp, _, f = wg.shape
    tm, tf = _tile(n, tm, SUBLANE), _tile(f, tf)
    te, n_used, src, aid = _moe_route(route, n_exp, tm)
    p_rows = src.shape[0]
    nf = f // tf
    assert tm % nf == 0
    frozen = lambda i, j, nu: jnp.where(i < nu[0], j, nf - 1)
    grid_spec = pltpu.PrefetchScalarGridSpec(
        num_scalar_prefetch=4,
        grid=(p_rows // tm, nf),
        in_specs=[pl.BlockSpec(memory_space=pl.ANY),
                  pl.BlockSpec((1, d, tf), lambda i, j, te, nu, s, a: (te[i], 0, frozen(i, j, nu))),
                  pl.BlockSpec((1, d, tf), lambda i, j, te, nu, s, a: (te[i], 0, frozen(i, j, nu))),
                  pl.BlockSpec((1, tf, d), lambda i, j, te, nu, s, a: (te[i], frozen(i, j, nu), 0))],
        out_specs=pl.BlockSpec(memory_space=pl.ANY),
        scratch_shapes=[pltpu.VMEM((2, tm, dh), jnp.uint32), pltpu.VMEM((tm, d), BF16), pltpu.VMEM((tm, d), F32),
                        pltpu.VMEM((2, tm, dh), jnp.uint32), pltpu.SemaphoreType.DMA((2,)),
                        pltpu.SemaphoreType.DMA((2,))],
    )
    return pl.pallas_call(
        functools.partial(_moe_kernel, nf=nf, p_rows=p_rows),
        out_shape=jax.ShapeDtypeStruct((p_rows + tm, dh), jnp.uint32),
        grid_spec=grid_spec,
        compiler_params=_cparams(("arbitrary", "arbitrary")),
        name="moe",
    )(te, n_used, src, aid, hpk, wg, wu, wd)


def _moe_finish_kernel(x_ref, y1_ref, y2_ref, r_ref, gate_ref, gy_ref, o_ref):
    r = r_ref[...]
    lane = lax.broadcasted_iota(jnp.int32, r.shape, 1)
    g1 = jnp.sum(jnp.where(lane == 2, r, 0.0), axis=-1, keepdims=True)
    g2 = jnp.sum(jnp.where(lane == 3, r, 0.0), axis=-1, keepdims=True)
    y = g1 * _unpack_halves(y1_ref[...]) + g2 * _unpack_halves(y2_ref[...])
    o_ref[...] = x_ref[...] + gate_ref[0] * _rms(y, gy_ref[...])


def _moe_finish(x, ypk, route, mod_row, gate, gy, *, tr=256):
    n, d = x.shape
    tr = _tile(n, tr, SUBLANE)
    nt = n // tr
    return pl.pallas_call(
        _moe_finish_kernel,
        out_shape=jax.ShapeDtypeStruct((n, d), F32),
        grid=(nt,),
        in_specs=[pl.BlockSpec((tr, d), lambda m: (m, 0)),
                  pl.BlockSpec((tr, d // 2), lambda m: (m, 0)),
                  pl.BlockSpec((tr, d // 2), lambda m: (nt + m, 0)),
                  pl.BlockSpec((tr, LANE), lambda m: (m, 0)),
                  pl.BlockSpec((1, 1, d), lambda m: (mod_row(m * tr), 0, 0)),
                  pl.BlockSpec((1, d), lambda m: (0, 0))],
        out_specs=pl.BlockSpec((tr, d), lambda m: (m, 0)),
        compiler_params=_cparams(("parallel",)),
        name="moe_finish",
    )(x, ypk, ypk, route, gate, gy.reshape(1, d))


def _ret_kernel(dec_ref, ql_ref, kl_ref, vl_ref, gl_ref, qc_ref, kc_ref, vc_ref, gc_ref, ca_ref, sa_ref,
                o_ref, qs, ks, kcs, af_l, ab_l, af_c, ab_c, st_f, st_b, *, chunk, kscale, unroll):
    hd = pl.program_id(1)
    s_len, dk = ql_ref.shape
    c_len = qc_ref.shape[0]
    dv = vl_ref.shape[1]
    ch = chunk
    ncl, ncc = s_len // ch, c_len // ch
    half = LANE // 2

    def sl(c):
        return pl.ds(pl.multiple_of(c * ch, ch), ch)

    lane = lax.broadcasted_iota(jnp.int32, (ch, LANE), 1)
    lo = lane < half
    sign = jnp.where(lo, -1.0, 1.0)

    def prep(c, _):
        r = sl(c)
        ca, sa = ca_ref[r, :], sa_ref[r, :]
        car, sar = pltpu.roll(ca, half, 1), pltpu.roll(sa, half, 1)
        cos = (jnp.where(lo, ca, car), jnp.where(lo, car, ca))
        sin = (sign * jnp.where(lo, sa, sar), sign * jnp.where(lo, sar, sa))

        def rope(x):
            parts = []
            for p in range(dk // LANE):
                xp = x[:, p * LANE:(p + 1) * LANE]
                parts.append(xp * cos[p] + pltpu.roll(xp, half, 1) * sin[p])
            return jnp.concatenate(parts, axis=-1)

        qs[r, :] = rope(ql_ref[r, :].astype(F32)).astype(BF16)
        ks[r, :] = (rope(kl_ref[r, :].astype(F32)) * kscale).astype(BF16)
        return 0

    lax.fori_loop(0, ncl, prep, 0)
    kcs[...] = (kc_ref[...].astype(F32) * kscale).astype(BF16)

    def log_gamma(dirn):
        d = jnp.full((1, 1), dec_ref[dirn, hd], F32)
        return -(jnp.maximum(-d, 0.0) + jnp.log1p(jnp.exp(-jnp.abs(d))))

    ii = lax.broadcasted_iota(jnp.int32, (ch, ch), 0).astype(F32)
    jj = lax.broadcasted_iota(jnp.int32, (ch, ch), 1).astype(F32)
    pos_k = lax.broadcasted_iota(jnp.int32, (ch, dk), 0).astype(F32)
    lg_f, lg_b = log_gamma(0), log_gamma(1)
    intra_f = jnp.where(ii >= jj, jnp.exp(lg_f * jnp.maximum(ii - jj, 0.0)), 0.0)
    intra_b = jnp.where(jj >= ii, jnp.exp(lg_b * jnp.maximum(jj - ii, 0.0)), 0.0)
    qdec_f = jnp.exp(lg_f * (pos_k + 1.0))
    kdec_f = jnp.exp(lg_f * (ch - 1.0 - pos_k))
    qdec_b = jnp.exp(lg_b * (ch - pos_k))
    kdec_b = jnp.exp(lg_b * pos_k)
    sdec_f = jnp.exp(lg_f * ch)
    sdec_b = jnp.exp(lg_b * ch)

    def step(q_src, k_src, v_src, acc, st, c, intra, qdec, kdec, sdec):
        r = sl(c)
        q, k, v = q_src[r, :], k_src[r, :], v_src[r, :]
        att = _dot_nt(q, k) * intra
        state = st[...]
        out = _dot(att.astype(BF16), v) + _dot((q.astype(F32) * qdec).astype(BF16), state.astype(BF16))
        acc[r, :] = out
        st[...] = state * sdec + _dot_tn((k.astype(F32) * kdec).astype(BF16), v)

    st_f[...] = jnp.zeros_like(st_f)
    st_b[...] = jnp.zeros_like(st_b)

    def scan(q_src, k_src, v_src, acc_f, acc_b, n):
        def body(c, _):
            step(q_src, k_src, v_src, acc_f, st_f, c, intra_f, qdec_f, kdec_f, sdec_f)
            step(q_src, k_src, v_src, acc_b, st_b, n - 1 - c, intra_b, qdec_b, kdec_b, sdec_b)
            return 0
        lax.fori_loop(0, n, body, 0, unroll=int(np.gcd(unroll, n)))

    scan(qc_ref, kcs, vc_ref, af_c, ab_c, ncc)
    scan(qs, ks, vl_ref, af_l, ab_l, ncl)

    def finish(acc_f, acc_b, g_ref, n, row0):
        def body(c, _):
            r = sl(c)
            y = acc_f[r, :] + acc_b[r, :]
            y = y * lax.rsqrt(jnp.mean(y * y, axis=-1, keepdims=True) + EPS)
            g = g_ref[r, :].astype(F32)
            dst = pl.ds(pl.multiple_of(row0 + c * ch, ch), ch)
            o_ref[dst, :] = (y * (g * jax.nn.sigmoid(g))).astype(o_ref.dtype)
            return 0
        lax.fori_loop(0, n, body, 0)

    finish(af_c, ab_c, gc_ref, ncc, 0)
    finish(af_l, ab_l, gl_ref, ncl, c_len)


def _retention(p0, ret_decay, rope_a, rope_b, b, s, c_len, dk, dv, *, unroll=8):
    nh = RET_HEADS
    assert dk == 2 * LANE and dv % LANE == 0 and s % RET_CHUNK == 0 and c_len % RET_CHUNK == 0
    assert s % c_len == 0
    cb0 = (b * s) // c_len
    assert dk == dv
    qo, ko, vo, go = 0, nh, 2 * nh, 3 * nh
    lat = lambda off: pl.BlockSpec((s, dk), lambda i, h, off=off: (i, off + h))
    ctx = lambda off: pl.BlockSpec((c_len, dk), lambda i, h, off=off: (cb0 + i, off + h))
    tab = pl.BlockSpec((s, LANE), lambda i, h: (0, 0))
    return pl.pallas_call(
        functools.partial(_ret_kernel, chunk=RET_CHUNK, kscale=dk ** -0.5, unroll=unroll),
        out_shape=jax.ShapeDtypeStruct((b * (s + c_len), nh * dv), BF16),
        grid=(b, nh),
        in_specs=[pl.BlockSpec(memory_space=pltpu.SMEM),
                  lat(qo), lat(ko), lat(vo), lat(go), ctx(qo), ctx(ko), ctx(vo), ctx(go), tab, tab],
        out_specs=pl.BlockSpec((s + c_len, dv), lambda i, h: (i, h)),
        scratch_shapes=[pltpu.VMEM((s, dk), BF16), pltpu.VMEM((s, dk), BF16), pltpu.VMEM((c_len, dk), BF16),
                        pltpu.VMEM((s, dv), F32), pltpu.VMEM((s, dv), F32),
                        pltpu.VMEM((c_len, dv), F32), pltpu.VMEM((c_len, dv), F32),
                        pltpu.VMEM((dk, dv), F32), pltpu.VMEM((dk, dv), F32)],
        compiler_params=_cparams(("parallel", "arbitrary")),
        name="retention",
    )(ret_decay.astype(F32), p0, p0, p0, p0, p0, p0, p0, p0, rope_a, rope_b)


def _na_kernel(q_ref, k_ref, v_ref, qc_ref, kc_ref, vc_ref, pt_ref, o_ref, *, qrows, krows, rows, w, scale):
    kh, g_rows = NA_KH, LANE // w
    nblk = rows // qrows
    kc, vc = kc_ref[...], vc_ref[...]
    c_len = kc_ref.shape[0]
    neg_tile = jnp.full((w, LANE), NEG_INF, F32)
    lane_row = lax.broadcasted_iota(jnp.int32, (w, LANE), 1) // w
    for blk in range(nblk):
        k0 = int(np.clip(blk * qrows - kh // 2, 0, rows - krows))
        bias_rows = []
        for i in range(qrows):
            qrow = blk * qrows + i
            win0 = int(np.clip(qrow - kh // 2, 0, rows - kh))
            tiles = []
            for jg in range(krows // g_rows):
                kr0 = k0 + jg * g_rows
                ok = [win0 <= kr0 + g < win0 + kh for g in range(g_rows)]
                if not any(ok):
                    tiles.append(neg_tile)
                    continue
                tile = pt_ref[0, kr0 - qrow + kh - 1 + g_rows - 1]
                if not all(ok):
                    lo, hi = ok.index(True), g_rows - ok[::-1].index(True)
                    tile = jnp.where((lane_row >= lo) & (lane_row < hi), tile, NEG_INF)
                tiles.append(tile)
            bias_rows.append(jnp.concatenate(tiles, axis=1))
        bias = jnp.concatenate(bias_rows, axis=0)
        qs = slice(blk * qrows * w, (blk + 1) * qrows * w)
        win = slice(k0 * w, (k0 + krows) * w)
        q = q_ref[qs, :]
        s_loc = _dot_nt(q, k_ref[win, :]) * scale + bias
        s_ctx = _dot_nt(q, kc) * scale
        m = jnp.maximum(jnp.max(s_loc, axis=-1, keepdims=True), jnp.max(s_ctx, axis=-1, keepdims=True))
        p_loc, p_ctx = jnp.exp(s_loc - m), jnp.exp(s_ctx - m)
        l = jnp.sum(p_loc, axis=-1, keepdims=True) + jnp.sum(p_ctx, axis=-1, keepdims=True)
        o = _dot(p_loc.astype(BF16), v_ref[win, :]) + _dot(p_ctx.astype(BF16), vc)
        o_ref[c_len + blk * qrows * w: c_len + (blk + 1) * qrows * w, :] = (o / l).astype(o_ref.dtype)

    sc = _dot_nt(qc_ref[...], kc) * scale
    pc = jnp.exp(sc - jnp.max(sc, axis=-1, keepdims=True))
    oc = _dot(pc.astype(BF16), vc) / jnp.sum(pc, axis=-1, keepdims=True)
    o_ref[0:c_len, :] = oc.astype(o_ref.dtype)


def _na_bias_tiles(rpb):
    w, kh, kw = GRID_W, NA_KH, NA_KW
    g_rows = LANE // w
    qc = np.arange(w)[:, None]
    kc = np.arange(w)[None, :]
    cstart = np.clip(qc - kw // 2, 0, w - kw)
    col_ok = (kc >= cstart) & (kc < cstart + kw)
    dc = np.clip(kc - qc + kw - 1, 0, 2 * kw - 2)
    per_dr = jnp.where(col_ok[None, None], rpb[:, :, dc], NEG_INF)
    n_a = 2 * kh - 1 + g_rows - 1
    idx = np.clip(np.arange(n_a)[:, None] - (g_rows - 1) + np.arange(g_rows)[None, :], 0, 2 * kh - 2)
    tiles = per_dr[:, idx]
    return jnp.transpose(tiles, (0, 1, 3, 2, 4)).reshape(rpb.shape[0], n_a, w, LANE).astype(F32)


def _neighbourhood(p0, rpb, b, s, c_len, hd, col0):
    nh, w, qr = NA_HEADS, GRID_W, NA_QROWS
    rows = s // w
    assert LANE % w == 0
    g_rows = LANE // w
    kr = -(-(qr + NA_KH - 1) // g_rows) * g_rows
    assert rows >= kr and rows % qr == 0 and rows // qr <= 16 and hd == LANE and s % c_len == 0
    cb0 = (b * s) // c_len
    qo, ko, vo = col0, col0 + nh, col0 + 2 * nh
    pt = _na_bias_tiles(rpb)
    lat = lambda off: pl.BlockSpec((s, hd), lambda i, h, off=off: (i, off + h))
    ctx = lambda off: pl.BlockSpec((c_len, hd), lambda i, h, off=off: (cb0 + i, off + h))
    return pl.pallas_call(
        functools.partial(_na_kernel, qrows=qr, krows=kr, rows=rows, w=w, scale=hd ** -0.5),
        out_shape=jax.ShapeDtypeStruct((b * (s + c_len), nh * hd), BF16),
        grid=(b, nh),
        in_specs=[lat(qo), lat(ko), lat(vo), ctx(qo), ctx(ko), ctx(vo),
                  pl.BlockSpec((1,) + pt.shape[1:], lambda i, h: (h, 0, 0, 0))],
        out_specs=pl.BlockSpec((s + c_len, hd), lambda i, h: (i, h)),
        compiler_params=_cparams(("parallel", "parallel")),
        name="neighbourhood",
    )(p0, p0, p0, p0, p0, p0, pt)


def _kpe_kernel(x_ref, t_ref, o_ref, *, n_lat):
    x = x_ref[...].astype(F32)
    half = LANE // 2
    lane = lax.broadcasted_iota(jnp.int32, x.shape, 1)
    r = x * t_ref[...]
    roped = r + pltpu.roll(r, half, 1)
    is_lat = pl.program_id(0) < n_lat
    o_ref[...] = jnp.where(lane < half, jnp.where(is_lat, roped, x), 0.0).astype(o_ref.dtype)


def _mla_kernel(qn_ref, qp_ref, tq_ref, kn_ref, kp_ref, v_ref, knc_ref, kpc_ref, vc_ref, o_ref, k_s, *, scale, tk):
    c_len, s_len = knc_ref.shape[0], kn_ref.shape[0]
    tq = qn_ref.shape[0]
    half = LANE // 2

    @pl.when(pl.program_id(2) == 0)
    def _():
        k_s[0:c_len, 0:LANE] = knc_ref[...]
        k_s[0:c_len, LANE:] = kpc_ref[...]
        k_s[c_len:, 0:LANE] = kn_ref[...]
        k_s[c_len:, LANE:] = kp_ref[...]

    c = scale * np.log2(np.e)
    r = qp_ref[...].astype(F32) * tq_ref[...]
    qpe = r + pltpu.roll(r, half, 1)
    q = jnp.concatenate([(qn_ref[...].astype(F32) * c).astype(BF16), (qpe * c).astype(BF16)], axis=-1)

    def update(k, v, carry):
        m, l, acc = carry
        s = _dot_nt(q, k)
        m_new = jnp.maximum(m, jnp.max(s, axis=-1, keepdims=True))
        alpha = jnp.exp2(m - m_new)
        p = jnp.exp2(s - jnp.tile(m_new, (1, s.shape[1] // LANE)))
        l = alpha * l + jnp.sum(p, axis=-1, keepdims=True)
        acc = alpha * acc + _dot(p.astype(BF16), v)
        return m_new, l, acc

    carry = (jnp.full((tq, LANE), NEG_INF, F32), jnp.zeros((tq, LANE), F32), jnp.zeros((tq, LANE), F32))
    carry = update(k_s[0:c_len, :], vc_ref[...], carry)
    for j in range(s_len // tk):
        carry = update(k_s[c_len + j * tk: c_len + (j + 1) * tk, :], v_ref[j * tk:(j + 1) * tk, :], carry)
    _, l, acc = carry
    o_ref[...] = (acc / l).astype(o_ref.dtype)


def _mla_attention(q, kv, kpe, tab, b, s, c_len, *, tq=512, tk=2048):
    nh = MLA_HEADS
    assert MLA_NOPE == LANE and MLA_V == LANE and 2 * MLA_ROPE == LANE and c_len % LANE == 0
    tq, tk = _tile(s, tq, SUBLANE), _tile(s, tk)
    nq = s // tq
    cb0 = (b * s) // c_len
    return pl.pallas_call(
        functools.partial(_mla_kernel, scale=(MLA_NOPE + MLA_ROPE) ** -0.5, tk=tk),
        out_shape=jax.ShapeDtypeStruct((b * s, nh * MLA_V), BF16),
        grid=(b, nh, nq),
        in_specs=[pl.BlockSpec((tq, LANE), lambda i, h, a: (i * nq + a, h)),
                  pl.BlockSpec((tq, LANE), lambda i, h, a: (i * nq + a, nh + h)),
                  pl.BlockSpec((tq, LANE), lambda i, h, a: (a, 0)),
                  pl.BlockSpec((s, LANE), lambda i, h, a: (i, h)),
                  pl.BlockSpec((s, LANE), lambda i, h, a: (i, 0)),
                  pl.BlockSpec((s, LANE), lambda i, h, a: (i, nh + h)),
                  pl.BlockSpec((c_len, LANE), lambda i, h, a: (cb0 + i, h)),
                  pl.BlockSpec((c_len, LANE), lambda i, h, a: (cb0 + i, 0)),
                  pl.BlockSpec((c_len, LANE), lambda i, h, a: (cb0 + i, nh + h))],
        out_specs=pl.BlockSpec((tq, MLA_V), lambda i, h, a: (i * nq + a, h)),
        scratch_shapes=[pltpu.VMEM((c_len + s, 2 * LANE), BF16)],
        compiler_params=_cparams(("parallel", "parallel", "arbitrary")),
        name="mla_attention",
    )(q, q, tab, kv, kpe, kv, kv, kpe, kv)


def _lru_kernel(xl_ref, xc_ref, yl_ref, cw_ref, cb_ref, wa_ref, wi_ref, ba_ref, bi_ref, lam_ref, o_ref,
                xp_l, xp_c, a_f, u_f, a_b, u_b, *, rows_per_step):
    s_len, c_len = xl_ref.shape[0], xc_ref.shape[0]
    pad = SUBLANE
    left = (LRU_CONV - 1) // 2
    cw, cb = cw_ref[...], cb_ref[...]

    def softplus(x):
        return jnp.maximum(x, 0.0) + jnp.log1p(jnp.exp(-jnp.abs(x)))

    sp = [softplus(-lam_ref[d:d + 1, :]) for d in range(2)]

    def gates(xp, n, f_off, b_off):
        step = min(rows_per_step, n)
        for r0 in range(0, n, step):
            x = cb
            for j in range(LRU_CONV):
                x = x + cw[j:j + 1, :] * xp[pad - left + j + r0: pad - left + j + r0 + step, :]
            xb = x.astype(BF16)

            def blockdiag(w_ref, d):
                return jnp.concatenate([_dot(xb[:, g * LANE:(g + 1) * LANE], w_ref[d, g])
                                        for g in range(w_ref.shape[1])], axis=-1)

            for d, (a_ref, u_ref, off) in enumerate(((a_f, u_f, f_off), (a_b, u_b, b_off))):
                rg = _sigmoid(blockdiag(wa_ref, d) + ba_ref[d:d + 1, :])
                ig = _sigmoid(blockdiag(wi_ref, d) + bi_ref[d:d + 1, :])
                log_a = -LRU_C * rg * sp[d]
                a_ref[off + r0: off + r0 + step, :] = jnp.exp(log_a)
                th = jnp.tanh(log_a)
                u_ref[off + r0: off + r0 + step, :] = jnp.sqrt(-2.0 * th / (1.0 - th)) * ig * x

    zeros = jnp.zeros((pad, xl_ref.shape[1]), F32)
    for xp, src, n in ((xp_l, xl_ref, s_len), (xp_c, xc_ref, c_len)):
        xp[0:pad, :] = zeros
        xp[pad + n: 2 * pad + n, :] = zeros
        xp[pad: pad + n, :] = src[...].astype(F32)
    gates(xp_c, c_len, 0, s_len)
    gates(xp_l, s_len, c_len, 0)

    n_tiles = (s_len + c_len) // SUBLANE
    row = lax.broadcasted_iota(jnp.int32, (SUBLANE, xl_ref.shape[1]), 0)

    def body(i, carry):
        hf, hb = carry
        rf = pl.ds(pl.multiple_of(i * SUBLANE, SUBLANE), SUBLANE)
        a, u = a_f[rf, :], u_f[rf, :]
        for sft in (1, 2, 4):
            m = row >= sft
            u = jnp.where(m, a * pltpu.roll(u, sft, 0) + u, u)
            a = jnp.where(m, a * pltpu.roll(a, sft, 0), a)
        h = a * hf + u
        u_f[rf, :] = h
        hf = h[SUBLANE - 1:SUBLANE, :]
        rb = pl.ds(pl.multiple_of((n_tiles - 1 - i) * SUBLANE, SUBLANE), SUBLANE)
        a, u = a_b[rb, :], u_b[rb, :]
        for sft in (1, 2, 4):
            m = row < SUBLANE - sft
            u = jnp.where(m, a * pltpu.roll(u, SUBLANE - sft, 0) + u, u)
            a = jnp.where(m, a * pltpu.roll(a, SUBLANE - sft, 0), a)
        h = a * hb + u
        u_b[rb, :] = h
        hb = h[0:1, :]
        return hf, hb

    h0 = jnp.zeros((1, xl_ref.shape[1]), F32)
    lax.fori_loop(0, n_tiles, body, (h0, h0), unroll=4)

    step = min(rows_per_step, s_len)
    for r0 in range(0, s_len, step):
        y = yl_ref[r0:r0 + step, :].astype(F32)
        h = u_f[c_len + r0: c_len + r0 + step, :] + u_b[r0:r0 + step, :]
        o_ref[r0:r0 + step, :] = (h * jax.nn.gelu(y)).astype(o_ref.dtype)


def _rglru(p1, conv_w, conv_b, wa, wi, ba, bi, lam, b, s, c_len, x_col0, y_col0, *, group=2):
    nblk, bd = wa.shape[1], wa.shape[2]
    width = nblk * bd
    group = int(np.gcd(np.gcd(group, nblk), np.gcd(x_col0, y_col0)))
    gw = group * bd
    assert bd == LANE and s % c_len == 0 and (s + c_len) % (4 * SUBLANE) == 0
    cb0 = (b * s) // c_len
    t = s + c_len
    xc, yc = x_col0 // group, y_col0 // group
    vec2 = pl.BlockSpec((2, gw), lambda i, k: (0, k))
    wspec = pl.BlockSpec((2, group, bd, bd), lambda i, k: (0, k, 0, 0))
    return pl.pallas_call(
        functools.partial(_lru_kernel, rows_per_step=512),
        out_shape=jax.ShapeDtypeStruct((b * s, width), BF16),
        grid=(b, nblk // group),
        in_specs=[pl.BlockSpec((s, gw), lambda i, k: (i, xc + k)),
                  pl.BlockSpec((c_len, gw), lambda i, k: (cb0 + i, xc + k)),
                  pl.BlockSpec((s, gw), lambda i, k: (i, yc + k)),
                  pl.BlockSpec((LRU_CONV, gw), lambda i, k: (0, k)),
                  pl.BlockSpec((1, gw), lambda i, k: (0, k)),
                  wspec, wspec, vec2, vec2, vec2],
        out_specs=pl.BlockSpec((s, gw), lambda i, k: (i, k)),
        scratch_shapes=[pltpu.VMEM((s + 2 * SUBLANE, gw), F32), pltpu.VMEM((c_len + 2 * SUBLANE, gw), F32),
                        pltpu.VMEM((t, gw), F32), pltpu.VMEM((t, gw), F32),
                        pltpu.VMEM((t, gw), F32), pltpu.VMEM((t, gw), F32)],
        compiler_params=_cparams(("parallel", "parallel")),
        name="rglru",
    )(p1, p1, p1, conv_w.astype(F32), conv_b.reshape(1, width).astype(F32),
      wa.astype(BF16), wi.astype(BF16), ba.astype(F32), bi.astype(F32), lam.astype(F32))


def _rope_angles(n, rot_dim):
    t = np.arange(n)
    row = (t // GRID_W).astype(np.float32)
    col = (t % GRID_W).astype(np.float32)
    nf = rot_dim // 4
    inv = jnp.asarray(ROPE_BASE, F32) ** (-jnp.arange(nf, dtype=F32) / nf)
    return row[:, None] * inv, col[:, None] * inv


def _swap_perm(rot_dim):
    q = rot_dim // 4
    idx = np.arange(rot_dim).reshape(4, q)
    return idx[[1, 0, 3, 2]].reshape(-1)


def kernel(x, c, ctx, c_ctx, w_mod, b_mod, norm_g, w_in_even, w_out_even, ret_decay, na_rpb, ffn_w_gate, ffn_w_up,
           ffn_w_down, w_in_odd, mla_q_norm, mla_w_uq, mla_kv_norm, mla_w_ukv, lru_conv_w, lru_conv_b, lru_w_a,
           lru_b_a, lru_w_i, lru_b_i, lru_lambda, w_out_odd, router_w, moe_w_gate, moe_w_up, moe_w_down):
    b, s, d = x.shape
    c_len = ctx.shape[1]
    n_lat, n_ctx = b * s, b * c_len
    n_tok = n_lat + n_ctx
    depth = w_mod.shape[0]
    assert depth == 2 and w_in_even.shape[0] == 1 and w_in_odd.shape[0] == 1
    assert n_ctx % SUBLANE == 0 and s % c_len == 0

    mod_rows = -(-(b + 1) // SUBLANE) * SUBLANE
    s_in = jnp.concatenate([c, c_ctx[None], jnp.zeros((mod_rows - b - 1, d), F32)], axis=0)
    mods = _modulation(s_in, w_mod, b_mod).reshape(depth, mod_rows, 6, 1, d)
    mod = lambda l, j: mods[l, :, j]
    mod_row = lambda r0: jnp.where(r0 < n_lat, r0 // s, b)


    g = norm_g[0]
    (h,) = _resnorm(x, n_tok, mod_row, x_ctx=ctx, gn=g[0], sc=mod(0, 0), sh=mod(0, 1))
    w_in = w_in_even[0].astype(BF16)
    p0 = _mm([(h, d, 0)], [(w_in, d, 0)], n_tok, w_in.shape[1], BF16, tn=1024, name="in_proj_even")
    ret_w = w_out_even.shape[1] - NA_HEADS * LANE
    dk = ret_w // RET_HEADS
    ar, ac = _rope_angles(s, dk)
    rope_a = jnp.concatenate([jnp.cos(ar), jnp.cos(ac)], axis=-1)
    rope_b = jnp.concatenate([jnp.sin(ar), jnp.sin(ac)], axis=-1)
    mix_a = _retention(p0, ret_decay[0], rope_a, rope_b, b, s, c_len, dk, dk)
    mix_b = _neighbourhood(p0, na_rpb[0], b, s, c_len, LANE, 4 * ret_w // LANE)
    w_out = w_out_even[0].astype(BF16)
    half_k = w_out.shape[0] // 2
    assert mix_a.shape[1] == half_k and mix_b.shape[1] == half_k
    o = _mm([(mix_a, half_k, 0), (mix_b, half_k, 0)], [(w_out, half_k, 0), (w_out, half_k, 1)],
            n_tok, d, BF16, tn=1024, name="out_proj_even")
    xs, h = _resnorm(x, n_tok, mod_row, x_ctx=ctx, y=o, y_batch_major=True, gate=mod(0, 2), gy=g[1], gn=g[2],
                     sc=mod(0, 3), sh=mod(0, 4))
    y = _ffn(h, ffn_w_gate[0].astype(BF16), ffn_w_up[0].astype(BF16), ffn_w_down[0].astype(BF16))
    g1 = norm_g[1]
    xs, h = _resnorm(xs, n_tok, mod_row, y=y, gate=mod(0, 5), gy=g[3], gn=g1[0], sc=mod(1, 0), sh=mod(1, 1))

    g = g1
    qr, kvr, rr = MLA_Q_RANK, MLA_KV_RANK, MLA_ROPE
    lw = lru_w_a.shape[2] * lru_w_a.shape[3]
    perm = _swap_perm(rr)
    wi = w_in_odd[0]
    o_q, o_kv, o_kr, o_lx, o_ly = 0, qr, qr + kvr, qr + kvr + rr, qr + kvr + rr + lw
    kr_w = wi[:, o_kr:o_kr + rr]
    w_in = jnp.concatenate([wi[:, o_q:o_q + qr], wi[:, o_kv:o_kv + kvr], wi[:, o_lx:o_lx + lw],
                            wi[:, o_ly:o_ly + lw], kr_w, kr_w[:, perm]], axis=-1).astype(BF16)
    assert qr % LANE == 0 and kvr % LANE == 0 and lw % LANE == 0 and 2 * rr == LANE
    p1 = _mm([(h, d, 0)], [(w_in, d, 0)], n_tok, w_in.shape[1], BF16, tn=896, name="in_proj_odd")
    nh = MLA_HEADS
    wq = mla_w_uq[0].reshape(qr, nh, MLA_NOPE + rr)
    wq_pe = wq[:, :, MLA_NOPE:]
    wq = jnp.concatenate([wq[:, :, :MLA_NOPE].reshape(qr, nh * MLA_NOPE),
                          jnp.concatenate([wq_pe, wq_pe[:, :, perm]], axis=-1).reshape(qr, nh * LANE)],
                         axis=-1).astype(BF16)
    wkv = mla_w_ukv[0].reshape(kvr, nh, MLA_NOPE + MLA_V)
    wkv = jnp.concatenate([wkv[:, :, :MLA_NOPE].reshape(kvr, nh * MLA_NOPE),
                           wkv[:, :, MLA_NOPE:].reshape(kvr, nh * MLA_V)], axis=-1).astype(BF16)
    assert qr % kvr == 0
    q = _mm([(p1, qr, 0)], [(wq, qr, 0)], n_lat, wq.shape[1], BF16, norm_g=mla_q_norm[0], tn=1024, name="mla_q")
    kv = _mm([(p1, kvr, qr // kvr)], [(wkv, kvr, 0)], n_tok, wkv.shape[1], BF16, norm_g=mla_kv_norm[0],
             name="mla_kv")
    ar, ac = _rope_angles(s, rr)
    cos4 = jnp.concatenate([jnp.cos(ar), jnp.cos(ar), jnp.cos(ac), jnp.cos(ac)], axis=-1)
    sin4 = jnp.concatenate([-jnp.sin(ar), jnp.sin(ar), -jnp.sin(ac), jnp.sin(ac)], axis=-1)
    tab = jnp.concatenate([cos4, sin4], axis=-1)
    kr_cb = (qr + kvr + 2 * lw) // LANE
    tr = _tile(s, 512, SUBLANE)
    assert n_ctx % tr == 0 or tr % n_ctx == 0
    tr = min(tr, n_ctx) if n_ctx % tr else tr
    nt_lat, nt_seq = n_lat // tr, s // tr
    kpe = pl.pallas_call(
        functools.partial(_kpe_kernel, n_lat=nt_lat),
        out_shape=jax.ShapeDtypeStruct((n_tok, LANE), BF16),
        grid=(n_tok // tr,),
        in_specs=[pl.BlockSpec((tr, LANE), lambda m: (m, kr_cb)),
                  pl.BlockSpec((tr, LANE), lambda m: (jnp.where(m < nt_lat, m % nt_seq, 0), 0))],
        out_specs=pl.BlockSpec((tr, LANE), lambda m: (m, 0)),
        compiler_params=_cparams(("parallel",)),
        name="mla_kpe",
    )(p1, tab)
    att = _mla_attention(q, kv, kpe, tab, b, s, c_len)
    lru = _rglru(p1, lru_conv_w[0], lru_conv_b[0], lru_w_a[0], lru_w_i[0], lru_b_a[0], lru_b_i[0], lru_lambda[0],
                 b, s, c_len, (qr + kvr) // LANE, (qr + kvr + lw) // LANE)
    w_out = w_out_odd[0].astype(BF16)
    half_k = w_out.shape[0] // 2
    assert att.shape[1] == half_k and lru.shape[1] == half_k
    o = _mm([(att, half_k, 0), (lru, half_k, 0)], [(w_out, half_k, 0), (w_out, half_k, 1)],
            n_lat, d, BF16, tn=1024, name="out_proj_odd")
    xl, hpk, route = _resnorm(xs, n_lat, mod_row, y=o, gate=mod(1, 2), gy=g[1], gn=g[2], sc=mod(1, 3),
                              sh=mod(1, 4), router_w=router_w[0])
    ypk = _moe(hpk, route, moe_w_gate[0].astype(BF16), moe_w_up[0].astype(BF16), moe_w_down[0].astype(BF16))
    xl = _moe_finish(xl, ypk, route, mod_row, mod(1, 5), g[3])
    return xl.reshape(b, s, d)
```
